```python
import math
import jax, jax.numpy as jnp
from jax import lax
import numpy as np

D_MODEL = 1024
BATCH = 2
SEQ = 8192
DEPTH = 2
DEC_BATCH = 32
DEC_SEQ = 4
PAST_LEN = 16384
PAGE_SIZE = 128

N_MIXERS = 2
N_LAYERS_A = (DEPTH + 1) // 2
N_LAYERS_B = DEPTH // 2
D_ATTN = D_MODEL
H_A = 16
DH_A = D_ATTN // H_A
MOBA_BLOCK = 256
MOBA_TOPK = 3
H_B = 8
DH_B = D_ATTN // (2 * H_B)
Q_BLOCK = 128
RMS_EPS = 1e-6
NEG = -1e30

kernel_name = "moba_diffattn_interleaved_hybrid_step"


def _rms_norm(x, g):
    xf = x.astype(jnp.float32)
    y = xf * lax.rsqrt(jnp.mean(xf * xf, axis=-1, keepdims=True) + RMS_EPS)
    return (y * g.astype(jnp.float32)).astype(x.dtype)


def _alibi_slopes(n_heads):
    return 2.0 ** (-8.0 * jnp.arange(1, n_heads + 1, dtype=jnp.float32) / n_heads)


def _lambda_init(layer_idx):
    return 0.8 - 0.6 * math.exp(-0.3 * layer_idx)


def _gather_pages(pool, j, page_table):
    g = pool[j, page_table]
    return g.reshape((g.shape[0], g.shape[1] * g.shape[2]) + g.shape[3:])


def _append_pad(past, new, multiple):
    parts = [] if past is None else [past]
    parts.append(new)
    total = sum(p.shape[1] for p in parts)
    pad = (-total) % multiple
    if pad:
        parts.append(jnp.zeros((new.shape[0], pad) + new.shape[2:], new.dtype))
    return parts[0] if len(parts) == 1 else jnp.concatenate(parts, axis=1)


def _sweep_queries(fn, q, q_pos):
    B, Tq = q.shape[:2]
    if Tq <= Q_BLOCK or Tq % Q_BLOCK:
        return fn(q, q_pos)
    n_c = Tq // Q_BLOCK
    qc = jnp.moveaxis(q.reshape((B, n_c, Q_BLOCK) + q.shape[2:]), 1, 0)
    pc = q_pos.reshape(n_c, Q_BLOCK)
    out = lax.map(lambda a: fn(a[0], a[1]), (qc, pc))
    return jnp.moveaxis(out, 0, 1).reshape((B, Tq) + out.shape[3:])


def _moba_attend(q, kb, vb, k_mean, q_pos, slopes):
    B, Tq, H, dh = q.shape
    NB = kb.shape[1]
    topk = min(MOBA_TOPK, NB)
    own = q_pos // MOBA_BLOCK
    gate = jnp.einsum("bqhd,bnhd->bqhn", q.astype(jnp.float32), k_mean)
    fully_past = jnp.arange(NB, dtype=jnp.int32)[None, :] < own[:, None]
    gate = jnp.where(fully_past[None, :, None, :], gate, NEG)
    top_val, top_idx = lax.top_k(gate, topk)
    sel_ok = top_val > 0.5 * NEG
    own_idx = jnp.broadcast_to(own[None, :, None, None], (B, Tq, H, 1)).astype(top_idx.dtype)
    idx = jnp.concatenate([top_idx, own_idx], axis=-1)
    ok = jnp.concatenate([sel_ok, jnp.ones((B, Tq, H, 1), dtype=bool)], axis=-1)
    b_i = jnp.arange(B)[:, None, None, None]
    h_i = jnp.arange(H)[None, None, :, None]
    k_sel = kb[b_i, idx, :, h_i, :]
    v_sel = vb[b_i, idx, :, h_i, :]
    s_pos = idx[..., None] * MOBA_BLOCK + jnp.arange(MOBA_BLOCK, dtype=jnp.int32)
    dist = q_pos[None, :, None, None, None] - s_pos
    scores = jnp.einsum("bqhd,bqhnkd->bqhnk", q, k_sel).astype(jnp.float32) * (DH_A ** -0.5)
    scores = scores - slopes[None, None, :, None, None] * dist.astype(jnp.float32)
    scores = jnp.where(ok[..., None] & (dist >= 0), scores, NEG)
    p = jax.nn.softmax(scores.reshape(B, Tq, H, -1), axis=-1).reshape(scores.shape)
    return jnp.einsum("bqhnk,bqhnkd->bqhd", p.astype(v_sel.dtype), v_sel)


def _moba_mix(q, k, v, k_past, v_past, slopes):
    past_len = 0 if k_past is None else k_past.shape[1]
    k_all = _append_pad(k_past, k, MOBA_BLOCK)
    v_all = _append_pad(v_past, v, MOBA_BLOCK)
    B, Tk, H, dh = k_all.shape
    kb = k_all.reshape(B, Tk // MOBA_BLOCK, MOBA_BLOCK, H, dh)
    vb = v_all.reshape(B, Tk // MOBA_BLOCK, MOBA_BLOCK, H, dh)
    k_mean = jnp.mean(kb.astype(jnp.float32), axis=2)
    q_pos = past_len + jnp.arange(q.shape[1], dtype=jnp.int32)
    return _sweep_queries(lambda qc, pc: _moba_attend(qc, kb, vb, k_mean, pc, slopes), q, q_pos)


def _diff_attend(q, k, v, q_pos, k_pos, slopes, lam):
    scores = jnp.einsum("bqhjd,bkhjd->bhjqk", q, k).astype(jnp.float32) * (DH_B ** -0.5)
    dist = q_pos[:, None] - k_pos[None, :]
    scores = scores - slopes[None, :, None, None, None] * dist.astype(jnp.float32)
    scores = jnp.where(dist >= 0, scores, NEG)
    p = jax.nn.softmax(scores, axis=-1)
    a = p[:, :, 0] - lam * p[:, :, 1]
    return jnp.einsum("bhqk,bkhe->bqhe", a.astype(v.dtype), v)


def _diff_mix(q, k, v, k_past, v_past, slopes, lam):
    past_len = 0 if k_past is None else k_past.shape[1]
    k_all = _append_pad(k_past, k, 1)
    v_all = _append_pad(v_past, v, 1)
    k_pos = jnp.arange(k_all.shape[1], dtype=jnp.int32)
    q_pos = past_len + jnp.arange(q.shape[1], dtype=jnp.int32)
    return _sweep_queries(lambda qc, pc: _diff_attend(qc, k_all, v_all, pc, k_pos, slopes, lam), q, q_pos)


def _trunk(x, page_table, cache_k_moba, cache_v_moba, cache_k_diff, cache_v_diff,
           norm_a, w_in_a, w_out_a, norm_b, w_in_b, w_out_b,
           lambda_q1, lambda_k1, lambda_q2, lambda_k2, subln_b, final_norm):
    B, T, _ = x.shape
    slopes_a = _alibi_slopes(H_A)
    slopes_b = _alibi_slopes(H_B)
    ka, va, kd, vd = [], [], [], []
    for i in range(DEPTH):
        j = i // N_MIXERS
        if i % N_MIXERS == 0:
            h = _rms_norm(x, norm_a[j])
            q, k, v, z = jnp.split(jnp.einsum("btd,de->bte", h, w_in_a[j]), 4, axis=-1)
            q = q.reshape(B, T, H_A, DH_A)
            k = k.reshape(B, T, H_A, DH_A)
            v = v.reshape(B, T, H_A, DH_A)
            if page_table is None:
                k_past, v_past = None, None
            else:
                k_past = _gather_pages(cache_k_moba, j, page_table)
                v_past = _gather_pages(cache_v_moba, j, page_table)
            o = _moba_mix(q, k, v, k_past, v_past, slopes_a).reshape(B, T, D_ATTN)
            ka.append(k)
            va.append(v)
            w_out = w_out_a[j]
        else:
            h = _rms_norm(x, norm_b[j])
            q, k, v, z = jnp.split(jnp.einsum("btd,de->bte", h, w_in_b[j]), 4, axis=-1)
            q = q.reshape(B, T, H_B, 2, DH_B)
            k = k.reshape(B, T, H_B, 2, DH_B)
            v = v.reshape(B, T, H_B, 2 * DH_B)
            lam_init = _lambda_init(i)
            lam = (jnp.exp(jnp.sum(lambda_q1[j].astype(jnp.float32) * lambda_k1[j].astype(jnp.float32)))
                   - jnp.exp(jnp.sum(lambda_q2[j].astype(jnp.float32) * lambda_k2[j].astype(jnp.float32)))
                   + lam_init)
            if page_table is None:
                k_past, v_past = None, None
            else:
                k_past = _gather_pages(cache_k_diff, j, page_table)
                v_past = _gather_pages(cache_v_diff, j, page_table)
            o = _diff_mix(q, k, v, k_past, v_past, slopes_b, lam)
            o = (_rms_norm(o, subln_b[j]) * (1.0 - lam_init)).reshape(B, T, D_ATTN)
            kd.append(k)
            vd.append(v)
            w_out = w_out_b[j]
        x = x + jnp.einsum("bte,ed->btd", o * jax.nn.silu(z), w_out)
    y = _rms_norm(x, final_norm)
    return y, jnp.stack(ka), jnp.stack(va), jnp.stack(kd), jnp.stack(vd)


def setup_inputs(seed: int = 0) -> dict:
    key = jax.random.key(seed)
    ks = jax.random.split(key, 20)
    f32 = jnp.float32
    n_pages = PAST_LEN // PAGE_SIZE
    n_used = DEC_BATCH * n_pages
    n_pool = n_used + max(1, n_used // 4)
    perm = jax.random.permutation(ks[0], n_pool)
    page_table = perm[:n_used].reshape(DEC_BATCH, n_pages).astype(jnp.int32)
    return {
        "x_prompt": jax.random.normal(ks[1], (BATCH, SEQ, D_MODEL), f32),
        "x_sample": jax.random.normal(ks[2], (DEC_BATCH, DEC_SEQ, D_MODEL), f32),
        "cache_k_moba": jax.random.normal(ks[3], (N_LAYERS_A, n_pool, PAGE_SIZE, H_A, DH_A), f32),
        "cache_v_moba": jax.random.normal(ks[4], (N_LAYERS_A, n_pool, PAGE_SIZE, H_A, DH_A), f32),
        "cache_k_diff": jax.random.normal(ks[5], (N_LAYERS_B, n_pool, PAGE_SIZE, H_B, 2, DH_B), f32),
        "cache_v_diff": jax.random.normal(ks[6], (N_LAYERS_B, n_pool, PAGE_SIZE, H_B, 2 * DH_B), f32),
        "page_table": page_table,
        "norm_a": 1.0 + 0.1 * jax.random.normal(ks[7], (N_LAYERS_A, D_MODEL), f32),
        "w_in_a": jax.random.normal(ks[8], (N_LAYERS_A, D_MODEL, 4 * D_ATTN), f32) * D_MODEL ** -0.5,
        "w_out_a": jax.random.normal(ks[9], (N_LAYERS_A, D_ATTN, D_MODEL), f32) * D_ATTN ** -0.5,
        "norm_b": 1.0 + 0.1 * jax.random.normal(ks[10], (N_LAYERS_B, D_MODEL), f32),
        "w_in_b": jax.random.normal(ks[11], (N_LAYERS_B, D_MODEL, 4 * D_ATTN), f32) * D_MODEL ** -0.5,
        "w_out_b": jax.random.normal(ks[12], (N_LAYERS_B, D_ATTN, D_MODEL), f32) * D_ATTN ** -0.5,
        "lambda_q1": 0.1 * jax.random.normal(ks[13], (N_LAYERS_B, DH_B), f32),
        "lambda_k1": 0.1 * jax.random.normal(ks[14], (N_LAYERS_B, DH_B), f32),
        "lambda_q2": 0.1 * jax.random.normal(ks[15], (N_LAYERS_B, DH_B), f32),
        "lambda_k2": 0.1 * jax.random.normal(ks[16], (N_LAYERS_B, DH_B), f32),
        "subln_b": 1.0 + 0.1 * jax.random.normal(ks[17], (N_LAYERS_B, 2 * DH_B), f32),
        "final_norm": 1.0 + 0.1 * jax.random.normal(ks[18], (D_MODEL,), f32),
    }


def reference(x_prompt, x_sample, cache_k_moba, cache_v_moba, cache_k_diff, cache_v_diff, page_table,
              norm_a, w_in_a, w_out_a, norm_b, w_in_b, w_out_b,
              lambda_q1, lambda_k1, lambda_q2, lambda_k2, subln_b, final_norm):
    y_prompt, k_moba_p, v_moba_p, k_diff_p, v_diff_p = _trunk(
        x_prompt, None, None, None, None, None,
        norm_a, w_in_a, w_out_a, norm_b, w_in_b, w_out_b,
        lambda_q1, lambda_k1, lambda_q2, lambda_k2, subln_b, final_norm)
    y_sample, k_moba_s, v_moba_s, k_diff_s, v_diff_s = _trunk(
        x_sample, page_table, cache_k_moba, cache_v_moba, cache_k_diff, cache_v_diff,
        norm_a, w_in_a, w_out_a, norm_b, w_in_b, w_out_b,
        lambda_q1, lambda_k1, lambda_q2, lambda_k2, subln_b, final_norm)
    return (y_prompt, y_sample, k_moba_p, v_moba_p, k_diff_p, v_diff_p, k_moba_s, v_moba_s, k_diff_s, v_diff_s)
```

```python
import functools
import math

import jax
import jax.numpy as jnp
from jax import lax
from jax.experimental import pallas as pl
from jax.experimental.pallas import tpu as pltpu

MOBA_BLOCK = 256
MOBA_TOPK = 3
RMS_EPS = 1e-6
NEG = -1e30

LANES = 128
ROW_TILE = 256
PAGES_PER_STEP = 8
VMEM_LIMIT = 56 * 1024 * 1024

F32 = jnp.float32
BF16 = jnp.bfloat16


def _dot(a, b):
    return jnp.dot(a, b, preferred_element_type=F32)


def _dot_nt(a, b):
    return lax.dot_general(a, b, (((1,), (1,)), ((), ())), preferred_element_type=F32)


def _split(x):
    hi = x.astype(BF16)
    lo = (x - hi.astype(F32)).astype(BF16)
    return hi, lo


def _dot_3pass(a, b, dot=_dot):
    a_hi, a_lo = _split(a)
    b_hi, b_lo = _split(b)
    return dot(a_hi, b_hi) + (dot(a_hi, b_lo) + dot(a_lo, b_hi))


def _rms(x, g):
    return x * lax.rsqrt(jnp.mean(x * x, axis=-1, keepdims=True) + RMS_EPS) * g


def _alibi_slopes(n_heads):
    return 2.0 ** (-8.0 * jnp.arange(1, n_heads + 1, dtype=F32) / n_heads)


def _lambda_init(layer_idx):
    return 0.8 - 0.6 * math.exp(-0.3 * layer_idx)


def _top_k_mask(gate, k):
    col = lax.broadcasted_iota(jnp.int32, gate.shape, 1).astype(F32)
    sel = jnp.zeros(gate.shape, F32)
    for _ in range(k):
        mx = jnp.max(gate, axis=1, keepdims=True)
        first = jnp.min(jnp.where(gate == mx, col, float(gate.shape[1])), axis=1, keepdims=True)
        hit = col == first
        sel = jnp.where(hit, jnp.where(mx > 0.5 * NEG, 1.0, 0.0), sel)
        gate = jnp.where(hit, -jnp.inf, gate)
    return sel


def _proj_kernel(*refs, has_branch, has_in_proj):
    refs = list(refs)
    if has_branch:
        o_ref, z_ref, x_ref, w_out_ref = refs[:4]
        refs = refs[4:]
        z = z_ref[...]
        u = o_ref[...] * (z / (1.0 + jnp.exp(-z)))
        x = x_ref[...] + _dot(u.astype(BF16), w_out_ref[...])
    else:
        x = refs[0][...]
        refs = refs[1:]
    g_ref = refs[0]
    h = _rms(x, g_ref[...])
    if not has_in_proj:
        refs[1][...] = h
        return
    w_in_ref = refs[1]
    outs = refs[2:]
    if has_branch:
        outs[0][...] = x
        outs = outs[1:]
    hb = h.astype(BF16)
    d = outs[0].shape[1]
    for j, out in enumerate(outs):
        out[...] = _dot(hb, w_in_ref[:, j * d:(j + 1) * d])


def _proj(x, g, *, branch=None, w_in=None):
    m, d = x.shape
    tm = min(ROW_TILE, m)
    assert m % tm == 0
    row = lambda i: (i, 0)
    fixed = lambda i: (0, 0)
    args, specs = [], []
    if branch is not None:
        o, z, w_out = branch
        args += [o, z, x, w_out.astype(BF16)]
        specs += [pl.BlockSpec((tm, o.shape[1]), row), pl.BlockSpec((tm, o.shape[1]), row),
                  pl.BlockSpec((tm, d), row), pl.BlockSpec(w_out.shape, fixed)]
    else:
        args.append(x)
        specs.append(pl.BlockSpec((tm, d), row))
    args.append(g.reshape(1, d))
    specs.append(pl.BlockSpec((1, d), fixed))
    if w_in is None:
        out_shape = jax.ShapeDtypeStruct((m, d), F32)
        out_specs = pl.BlockSpec((tm, d), row)
    else:
        args.append(w_in.astype(BF16))
        specs.append(pl.BlockSpec(w_in.shape, fixed))
        n_out = 4 + (branch is not None)
        da = w_in.shape[1] // 4
        widths = ([d] if branch is not None else []) + [da] * 4
        out_shape = tuple(jax.ShapeDtypeStruct((m, w), F32) for w in widths)
        out_specs = tuple(pl.BlockSpec((tm, w), row) for w in widths)
        assert len(out_shape) == n_out
    return pl.pallas_call(
        functools.partial(_proj_kernel, has_branch=branch is not None, has_in_proj=w_in is not None),
        grid=(m // tm,),
        in_specs=specs,
        out_specs=out_specs,
        out_shape=out_shape,
        compiler_params=pltpu.CompilerParams(dimension_semantics=("arbitrary",), vmem_limit_bytes=VMEM_LIMIT),
        name="proj",
    )(*args)


def _prompt_attn_kernel(*refs, moba, n_blocks, scale, lam_init):
    if moba:
        slopes_ref, q_ref, k_ref, v_ref, o_ref, kb, vb, km = refs
    else:
        slopes_ref, lam_ref, subln_ref, q_ref, k_ref, v_ref, o_ref, kb, vb = refs
    blk = MOBA_BLOCK
    grp = pl.program_id(1)
    i = pl.program_id(2)
    half = LANES // 2

    @pl.when(i == 0)
    def _():
        for n in range(n_blocks):
            kblk = k_ref[n * blk:(n + 1) * blk, :]
            kb[n * blk:(n + 1) * blk, :] = kblk.astype(BF16)
            vb[n * blk:(n + 1) * blk, :] = v_ref[n * blk:(n + 1) * blk, :].astype(BF16)
            if moba:
                km[n:n + 1, :] = jnp.sum(kblk, axis=0, keepdims=True) * (1.0 / blk)

    q = q_ref[...]
    lane = lax.broadcasted_iota(jnp.int32, (1, LANES), 1)
    lane_masks = (lane < half, lane >= half)
    r_idx = lax.broadcasted_iota(jnp.int32, (blk, blk), 0)
    c_idx = lax.broadcasted_iota(jnp.int32, (blk, blk), 1)
    causal = c_idx <= r_idx
    rel = (r_idx - c_idx).astype(F32)

    qs, bases, slopes, sels = [], [], [], []
    for e in range(2):
        slope = slopes_ref[2 * grp + e] if moba else slopes_ref[grp]
        qe = jnp.where(lane_masks[e], q, 0.0)
        qs.append((qe * scale).astype(BF16))
        slopes.append(slope)
        bases.append(slope * rel)
        if moba:
            gate = _dot_3pass(qe, km[...], _dot_nt)
            past = lax.broadcasted_iota(jnp.int32, gate.shape, 1) < i
            sels.append(_top_k_mask(jnp.where(past, gate, NEG), min(MOBA_TOPK, n_blocks)).astype(BF16))

    def attend(n, state, mask_fn):
        kn = kb[pl.ds(pl.multiple_of(n * blk, blk), blk), :]
        vn = vb[pl.ds(pl.multiple_of(n * blk, blk), blk), :]
        shift = ((i - n) * blk).astype(F32)
        new_state = []
        for e in range(2):
            s = _dot_nt(qs[e], kn) - bases[e] - slopes[e] * shift
            s = mask_fn(e, s)
            s_max = jnp.max(s, axis=1, keepdims=True)
            if state is None:
                m_new = s_max
                p = jnp.exp(s - m_new)
                l_new = jnp.sum(p, axis=1, keepdims=True)
                acc_new = _dot(p.astype(BF16), vn)
            else:
                m_old, l_old, acc_old = state[e]
                m_new = jnp.maximum(m_old, s_max)
                alpha = jnp.exp(m_old - m_new)
                p = jnp.exp(s - m_new)
                l_new = alpha * l_old + jnp.sum(p, axis=1, keepdims=True)
                acc_new = alpha * acc_old + _dot(p.astype(BF16), vn)
            new_state.append((m_new, l_new, acc_new))
        return tuple(new_state)

    state = attend(i, None, lambda e, s: jnp.where(causal, s, NEG))

    if moba:
        blk_row = lax.broadcasted_iota(jnp.int32, (n_blocks, blk), 0)

        def past_mask(n):
            expand = jnp.where(blk_row == n, 1.0, 0.0).astype(BF16)
            return lambda e, s: jnp.where(_dot(sels[e], expand) > 0.5, s, NEG)
    else:
        def past_mask(n):
            return lambda e, s: s

    state = lax.fori_loop(0, i, lambda n, st: attend(n, st, past_mask(n)), state)

    outs = [acc / l for (_, l, acc) in state]
    if moba:
        o_ref[...] = jnp.where(lane_masks[0], outs[0], outs[1])
    else:
        o = outs[0] - lam_ref[...] * outs[1]
        o_ref[...] = _rms(o, subln_ref[...]) * (1.0 - lam_init)


def _lambda_kernel(q1_ref, k1_ref, q2_ref, k2_ref, lam_ref, *, lam_init):
    a = jnp.sum(q1_ref[...] * k1_ref[...], axis=1, keepdims=True)
    b = jnp.sum(q2_ref[...] * k2_ref[...], axis=1, keepdims=True)
    lam_ref[...] = jnp.exp(a) - jnp.exp(b) + lam_init


def _lambda(lq1, lk1, lq2, lk2, lam_init):
    args = [v.reshape(1, -1) for v in (lq1, lk1, lq2, lk2)]
    return pl.pallas_call(
        functools.partial(_lambda_kernel, lam_init=lam_init),
        out_shape=jax.ShapeDtypeStruct((1, 1), F32),
        name="lambda",
    )(*args)


def _prompt_attn(q, k, v, slopes, *, batch, moba, lam=None, subln=None, lam_init=0.0):
    m, da = q.shape
    seq = m // batch
    blk = MOBA_BLOCK
    assert seq % blk == 0 and da % LANES == 0
    n_groups = da // LANES
    n_blocks = seq // blk
    q_spec = pl.BlockSpec((blk, LANES), lambda b, g, i: (b * n_blocks + i, g))
    kv_spec = pl.BlockSpec((seq, LANES), lambda b, g, i: (b, g))
    smem = pl.BlockSpec(memory_space=pltpu.SMEM)
    args, specs = [slopes], [smem]
    scratch = [pltpu.VMEM((seq, LANES), BF16), pltpu.VMEM((seq, LANES), BF16)]
    if moba:
        scratch.append(pltpu.VMEM((n_blocks, LANES), F32))
    else:
        args += [lam, subln.reshape(1, LANES)]
        specs += [pl.BlockSpec((1, 1), lambda b, g, i: (0, 0)), pl.BlockSpec((1, LANES), lambda b, g, i: (0, 0))]
    args += [q, k, v]
    specs += [q_spec, kv_spec, kv_spec]
    return pl.pallas_call(
        functools.partial(_prompt_attn_kernel, moba=moba, n_blocks=n_blocks, scale=(LANES // 2) ** -0.5,
                          lam_init=lam_init),
        grid=(batch, n_groups, n_blocks),
        in_specs=specs,
        out_specs=q_spec,
        out_shape=jax.ShapeDtypeStruct((m, da), F32),
        scratch_shapes=scratch,
        compiler_params=pltpu.CompilerParams(dimension_semantics=("arbitrary",) * 3, vmem_limit_bytes=VMEM_LIMIT),
        name="moba_prompt" if moba else "diff_prompt",
    )(*args)


def _paged_attn_kernel(*refs, moba, v_token_minor, n_chunks, n_q, past_len, page, lam_init):
    pps = PAGES_PER_STEP
    pt_ref, slopes_ref = refs[:2]
    refs = refs[2:]
    if not moba:
        lam_ref, subln_ref = refs[:2]
        refs = refs[2:]
    q_ref, kn_ref, vn_ref = refs[:3]
    k_pages = refs[3:3 + pps]
    v_pages = refs[3 + pps:3 + 2 * pps]
    refs = refs[3 + 2 * pps:]
    o_ref, qbd_scr, s_scr, p_scr, pown_scr, acc_scr = refs[:6]
    if moba:
        kmean_scr = refs[6]
    del pt_ref
    step = pl.program_id(1)
    d = q_ref.shape[2]
    half = LANES // 2
    n_grp = d // half
    rows = n_q * n_grp
    chunk = pps * page
    pages_per_blk = MOBA_BLOCK // page
    grp_of_lane = lax.broadcasted_iota(jnp.int32, (n_grp, d), 1) // half
    grp_of_row = lax.broadcasted_iota(jnp.int32, (n_grp, d), 0)
    diag = grp_of_lane == grp_of_row

    @pl.when(step == 0)
    def _():
        q = q_ref[0]
        for t in range(n_q):
            qbd_scr[t * n_grp:(t + 1) * n_grp, :] = jnp.where(diag, jnp.broadcast_to(q[t:t + 1, :], (n_grp, d)), 0.0)
        if moba:
            kmean_scr[...] = jnp.zeros(kmean_scr.shape, F32)

    @pl.when(step < n_chunks)
    def _():
        pages = [ref[0, 0] for ref in k_pages]
        kc = jnp.concatenate([pg.astype(BF16) for pg in pages], axis=1)
        scale = half ** -0.5
        s_scr[step] = _dot((qbd_scr[...] * scale).astype(BF16), kc)
        if moba:
            blk_lane = lax.broadcasted_iota(jnp.int32, (1, LANES), 1)
            for a in range(pps // pages_per_blk):
                tot = pages[a * pages_per_blk]
                for b in range(1, pages_per_blk):
                    tot = tot + pages[a * pages_per_blk + b]
                mean = jnp.sum(tot, axis=1, keepdims=True) * (1.0 / MOBA_BLOCK)
                n = step * (pps // pages_per_blk) + a
                kmean_scr[...] = jnp.where(blk_lane == n, mean, kmean_scr[...])

    @pl.when(step == n_chunks - 1)
    def _():
        qbd = qbd_scr[...]
        row = lax.broadcasted_iota(jnp.int32, (rows, 1), 0)
        q_pos = past_len + row // n_grp
        slope = jnp.zeros((rows, 1), F32)
        for g in range(n_grp):
            s_g = slopes_ref[g] if moba else slopes_ref[g // 2]
            slope = jnp.where(row % n_grp == g, s_g, slope)
        key_in_chunk = lax.broadcasted_iota(jnp.int32, (rows, chunk), 1)
        if moba:
            n_blk = past_len // MOBA_BLOCK
            gate = _dot_3pass(qbd, kmean_scr[...])
            is_blk = lax.broadcasted_iota(jnp.int32, gate.shape, 1) < n_blk
            sel = _top_k_mask(jnp.where(is_blk, gate, NEG), min(MOBA_TOPK, n_blk)).astype(BF16)
            blk_per_chunk = chunk // MOBA_BLOCK
            e_row = lax.broadcasted_iota(jnp.int32, (LANES, chunk), 0)
            e_col = lax.broadcasted_iota(jnp.int32, (LANES, chunk), 1) // MOBA_BLOCK
        kn = kn_ref[0]
        s_own = []
        for t in range(n_q):
            s_t = jnp.sum(qbd * kn[t:t + 1, :], axis=1, keepdims=True) * (half ** -0.5)
            s_t = s_t - slope * (q_pos - (past_len + t)).astype(F32)
            s_own.append(jnp.where(q_pos >= past_len + t, s_t, NEG))
        m = s_own[0]
        for t in range(1, n_q):
            m = jnp.maximum(m, s_own[t])

        def biased(c):
            s = s_scr[c] - slope * (q_pos - (c * chunk + key_in_chunk)).astype(F32)
            if moba:
                expand = jnp.where(e_row == c * blk_per_chunk + e_col, 1.0, 0.0).astype(BF16)
                s = jnp.where(_dot(sel, expand) > 0.5, s, NEG)
            return s

        def max_body(c, m):
            s = biased(c)
            s_scr[c] = s
            return jnp.maximum(m, jnp.max(s, axis=1, keepdims=True))

        m = lax.fori_loop(0, n_chunks, max_body, m)

        def sum_body(c, l):
            p = jnp.exp(s_scr[c] - m)
            s_scr[c] = p
            return l + jnp.sum(p, axis=1, keepdims=True)

        p_own = [jnp.exp(s_t - m) for s_t in s_own]
        l = p_own[0]
        for t in range(1, n_q):
            l = l + p_own[t]
        l = lax.fori_loop(0, n_chunks, sum_body, l)
        inv = 1.0 / l

        def norm_body(c, carry):
            p_scr[c] = (s_scr[c] * inv).astype(BF16)
            return carry

        lax.fori_loop(0, n_chunks, norm_body, 0)
        for t in range(n_q):
            pown_scr[:, t:t + 1] = p_own[t] * inv

    @pl.when(step >= n_chunks)
    def _():
        if v_token_minor:
            vc = jnp.concatenate([ref[0, 0].astype(BF16) for ref in v_pages], axis=1)
            contrib = _dot_nt(p_scr[step - n_chunks], vc)
        else:
            n_head = d // LANES
            vc = jnp.concatenate(
                [jnp.concatenate([ref[0, 0, pl.ds(h, page, stride=n_head), :].astype(BF16) for h in range(n_head)],
                                 axis=1) for ref in v_pages], axis=0)
            contrib = _dot(p_scr[step - n_chunks], vc)

        @pl.when(step == n_chunks)
        def _():
            acc_scr[...] = contrib

        @pl.when(step > n_chunks)
        def _():
            acc_scr[...] += contrib

    @pl.when(step == 2 * n_chunks - 1)
    def _():
        vn = vn_ref[0]
        acc = acc_scr[...]
        for t in range(n_q):
            acc = acc + pown_scr[:, t:t + 1] * vn[t:t + 1, :]
        if moba:
            for t in range(n_q):
                o_ref[0, t:t + 1, :] = jnp.sum(jnp.where(diag, acc[t * n_grp:(t + 1) * n_grp, :], 0.0),
                                               axis=0, keepdims=True)
        else:
            lane_head = lax.broadcasted_iota(jnp.int32, (n_grp, d), 1) // LANES
            row_grp = lax.broadcasted_iota(jnp.int32, (n_grp, d), 0)
            coef = jnp.where(row_grp % 2 == 0, 1.0, -lam_ref[...])
            coef = jnp.where(lane_head == row_grp // 2, coef, 0.0)
            subln = subln_ref[...]
            for t in range(n_q):
                o_t = jnp.sum(acc[t * n_grp:(t + 1) * n_grp, :] * coef, axis=0, keepdims=True)
                for h in range(d // LANES):
                    seg = o_t[:, h * LANES:(h + 1) * LANES]
                    o_ref[0, t:t + 1, h * LANES:(h + 1) * LANES] = _rms(seg, subln) * (1.0 - lam_init)


def _paged_attn(q, k_new, v_new, cache_k, cache_v, layer, page_table, slopes, *, moba, v_token_minor, lam=None,
                subln=None, lam_init=0.0):
    bd, n_q, d = q.shape
    page = cache_k.shape[3]
    n_pages = page_table.shape[1]
    past_len = n_pages * page
    pps = PAGES_PER_STEP
    assert n_pages % pps == 0 and MOBA_BLOCK % page == 0 and pps % (MOBA_BLOCK // page) == 0
    assert past_len % MOBA_BLOCK == 0 and n_q <= MOBA_BLOCK
    n_chunks = n_pages // pps
    chunk = pps * page
    half = LANES // 2
    n_grp = d // half
    rows = n_q * n_grp

    tok_spec = pl.BlockSpec((1, n_q, d), lambda b, s, pt: (b, 0, 0))

    def k_page_spec(p):
        return pl.BlockSpec((1, 1, d, page),
                            lambda b, s, pt: (layer, pt[b, jnp.minimum(s, n_chunks - 1) * pps + p], 0, 0))

    def v_page_spec(p):
        return pl.BlockSpec((1, 1, d, page) if v_token_minor else (1, 1, page * (d // LANES), LANES),
                            lambda b, s, pt: (layer, pt[b, jnp.maximum(s - n_chunks, 0) * pps + p], 0, 0))

    smem = pl.BlockSpec(memory_space=pltpu.SMEM)
    args, specs = [slopes], [smem]
    if not moba:
        args += [lam, subln.reshape(1, LANES)]
        specs += [pl.BlockSpec((1, 1), lambda b, s, pt: (0, 0)), pl.BlockSpec((1, LANES), lambda b, s, pt: (0, 0))]
    args += [q, k_new, v_new] + [cache_k] * pps + [cache_v] * pps
    specs += [tok_spec] * 3 + [k_page_spec(p) for p in range(pps)] + [v_page_spec(p) for p in range(pps)]
    scratch = [
        pltpu.VMEM((rows, d), F32),
        pltpu.VMEM((n_chunks, rows, chunk), F32),
        pltpu.VMEM((n_chunks, rows, chunk), BF16),
        pltpu.VMEM((rows, LANES), F32),
        pltpu.VMEM((rows, d), F32),
    ]
    if moba:
        assert past_len // MOBA_BLOCK <= LANES
        scratch.append(pltpu.VMEM((d, LANES), F32))
    grid_spec = pltpu.PrefetchScalarGridSpec(
        num_scalar_prefetch=1,
        grid=(bd, 2 * n_chunks),
        in_specs=specs,
        out_specs=tok_spec,
        scratch_shapes=scratch,
    )
    return pl.pallas_call(
        functools.partial(_paged_attn_kernel, moba=moba, v_token_minor=v_token_minor, n_chunks=n_chunks, n_q=n_q,
                          past_len=past_len, page=page, lam_init=lam_init),
        grid_spec=grid_spec,
        out_shape=jax.ShapeDtypeStruct((bd, n_q, d), F32),
        compiler_params=pltpu.CompilerParams(dimension_semantics=("arbitrary",) * 2, vmem_limit_bytes=VMEM_LIMIT),
        name="moba_sample" if moba else "diff_sample",
    )(page_table, *args)


def _trunk(x, page_table, caches, weights):
    (norm_a, w_in_a, w_out_a, norm_b, w_in_b, w_out_b, lq1, lk1, lq2, lk2, subln_b, final_norm) = weights
    cache_k_moba, cache_v_moba, cache_k_diff, cache_v_diff = caches
    bsz, t, d = x.shape
    depth = w_in_a.shape[0] + w_in_b.shape[0]
    h_a = d // (LANES // 2)
    h_b = d // LANES
    slopes_a = _alibi_slopes(h_a)
    slopes_b = _alibi_slopes(h_b)
    xf = x.reshape(bsz * t, d)
    ka, va, kd, vd = [], [], [], []
    branch = None
    for i in range(depth):
        j = i // 2
        moba = i % 2 == 0
        norm, w_in, w_out = (norm_a[j], w_in_a[j], w_out_a[j]) if moba else (norm_b[j], w_in_b[j], w_out_b[j])
        outs = _proj(xf, norm, branch=branch, w_in=w_in)
        if branch is not None:
            xf = outs[0]
            outs = outs[1:]
        q, k, v, z = outs
        lam_init = _lambda_init(i)
        kwargs = {}
        if not moba:
            kwargs = dict(lam=_lambda(lq1[j], lk1[j], lq2[j], lk2[j], lam_init), subln=subln_b[j], lam_init=lam_init)
        slopes = slopes_a if moba else slopes_b
        if page_table is None:
            o = _prompt_attn(q, k, v, slopes, batch=bsz, moba=moba, **kwargs)
        else:
            ck, cv = (cache_k_moba, cache_v_moba) if moba else (cache_k_diff, cache_v_diff)
            n_pool, page = ck.shape[1], ck.shape[2]
            v_token_minor = cv.shape[-1] < LANES
            ck = jnp.swapaxes(ck.reshape(ck.shape[0], n_pool, page, d), 2, 3)
            if v_token_minor:
                cv = jnp.swapaxes(cv.reshape(cv.shape[0], n_pool, page, d), 2, 3)
            else:
                assert cv.shape[-1] == LANES
                cv = cv.reshape(cv.shape[0], n_pool, page * (d // LANES), LANES)
            o = _paged_attn(q.reshape(bsz, t, d), k.reshape(bsz, t, d), v.reshape(bsz, t, d), ck, cv,
                            j, page_table, slopes, moba=moba, v_token_minor=v_token_minor,
                            **kwargs).reshape(bsz * t, d)
        if moba:
            ka.append(k.reshape(bsz, t, h_a, d // h_a))
            va.append(v.reshape(bsz, t, h_a, d // h_a))
        else:
            kd.append(k.reshape(bsz, t, h_b, 2, d // (2 * h_b)))
            vd.append(v.reshape(bsz, t, h_b, d // h_b))
        branch = (o, z, w_out)
    y = _proj(xf, final_norm, branch=branch).reshape(bsz, t, d)
    return y, jnp.stack(ka), jnp.stack(va), jnp.stack(kd), jnp.stack(vd)


def kernel(x_prompt, x_sample, cache_k_moba, cache_v_moba, cache_k_diff, cache_v_diff, page_table, norm_a, w_in_a, w_out_a, norm_b, w_in_b, w_out_b, lambda_q1, lambda_k1, lambda_q2, lambda_k2, subln_b, final_norm):
    weights = (norm_a, w_in_a, w_out_a, norm_b, w_in_b, w_out_b, lambda_q1, lambda_k1, lambda_q2, lambda_k2,
               subln_b, final_norm)
    caches = (cache_k_moba, cache_v_moba, cache_k_diff, cache_v_diff)
    y_p, ka_p, va_p, kd_p, vd_p = _trunk(x_prompt, None, caches, weights)
    y_s, ka_s, va_s, kd_s, vd_s = _trunk(x_sample, page_table, caches, weights)
    return (y_p, y_s, ka_p, va_p, kd_p, vd_p, ka_s, va_s, kd_s, vd_s)
```

```python
import functools
import math

import jax
import jax.numpy as jnp
from jax import lax
from jax.experimental import pallas as pl
from jax.experimental.pallas import tpu as pltpu

MOBA_BLOCK = 256
MOBA_TOPK = 3
RMS_EPS = 1e-6
NEG = -1e30
LOG2E = math.log2(math.e)

LANES = 128
ROW_TILE = 256
KEY_GROUP = 2
N_SLOPE_PARTS = 3
ONES_ROWS = 16
PAGES_PER_STEP = 16
VMEM_LIMIT = 56 * 1024 * 1024

F32 = jnp.float32
BF16 = jnp.bfloat16


def _dot(a, b):
    return jnp.dot(a, b, preferred_element_type=F32)


def _dot_nt(a, b):
    return lax.dot_general(a, b, (((1,), (1,)), ((), ())), preferred_element_type=F32)


def _split(x):
    hi = x.astype(BF16)
    lo = (x - hi.astype(F32)).astype(BF16)
    return hi, lo


def _dot_3pass(a, b, dot=_dot):
    a_hi, a_lo = _split(a)
    b_hi, b_lo = _split(b)
    return dot(a_hi, b_hi) + (dot(a_hi, b_lo) + dot(a_lo, b_hi))


def _rms(x, g):
    return x * lax.rsqrt(jnp.mean(x * x, axis=-1, keepdims=True) + RMS_EPS) * g


def _alibi_slopes(n_heads):
    return 2.0 ** (-8.0 * jnp.arange(1, n_heads + 1, dtype=F32) / n_heads)


def _lambda_init(layer_idx):
    return 0.8 - 0.6 * math.exp(-0.3 * layer_idx)


def _reduce_rows(x, op):
    rows = x.shape[0]
    while rows > 8 and rows % 2 == 0:
        rows //= 2
        x = op(x[:rows], x[rows:])
    reduce = jnp.max if op is jnp.maximum else jnp.sum
    return reduce(x, axis=0, keepdims=True)


def _top_k_mask(gate, k, axis=1):
    col = lax.broadcasted_iota(jnp.int32, gate.shape, axis).astype(F32)
    sel = jnp.zeros(gate.shape, F32)
    for _ in range(k):
        mx = jnp.max(gate, axis=axis, keepdims=True)
        first = jnp.min(jnp.where(gate == mx, col, float(gate.shape[axis])), axis=axis, keepdims=True)
        hit = col == first
        sel = jnp.where(hit, jnp.where(mx > 0.5 * NEG, 1.0, 0.0), sel)
        gate = jnp.where(hit, -jnp.inf, gate)
    return sel


def _proj_kernel(*refs, has_branch, has_in_proj):
    refs = list(refs)
    if has_branch:
        o_ref, z_ref, x_ref, w_out_ref = refs[:4]
        refs = refs[4:]
        z = z_ref[...]
        u = o_ref[...] * (z / (1.0 + jnp.exp(-z)))
        x = x_ref[...] + _dot(u.astype(BF16), w_out_ref[...])
    else:
        x = refs[0][...]
        refs = refs[1:]
    g_ref = refs[0]
    h = _rms(x, g_ref[...])
    if not has_in_proj:
        refs[1][...] = h
        return
    w_in_ref = refs[1]
    outs = refs[2:]
    if has_branch:
        outs[0][...] = x
        outs = outs[1:]
    hb = h.astype(BF16)
    d = outs[0].shape[1]
    for j, out in enumerate(outs):
        out[...] = _dot(hb, w_in_ref[:, j * d:(j + 1) * d])


def _proj(x, g, *, branch=None, w_in=None):
    m, d = x.shape
    tm = min(ROW_TILE, m)
    assert m % tm == 0
    row = lambda i: (i, 0)
    fixed = lambda i: (0, 0)
    args, specs = [], []
    if branch is not None:
        o, z, w_out = branch
        args += [o, z, x, w_out.astype(BF16)]
        specs += [pl.BlockSpec((tm, o.shape[1]), row), pl.BlockSpec((tm, o.shape[1]), row),
                  pl.BlockSpec((tm, d), row), pl.BlockSpec(w_out.shape, fixed)]
    else:
        args.append(x)
        specs.append(pl.BlockSpec((tm, d), row))
    args.append(g.reshape(1, d))
    specs.append(pl.BlockSpec((1, d), fixed))
    if w_in is None:
        out_shape = jax.ShapeDtypeStruct((m, d), F32)
        out_specs = pl.BlockSpec((tm, d), row)
    else:
        args.append(w_in.astype(BF16))
        specs.append(pl.BlockSpec(w_in.shape, fixed))
        n_out = 4 + (branch is not None)
        da = w_in.shape[1] // 4
        widths = ([d] if branch is not None else []) + [da] * 4
        out_shape = tuple(jax.ShapeDtypeStruct((m, w), F32) for w in widths)
        out_specs = tuple(pl.BlockSpec((tm, w), row) for w in widths)
        assert len(out_shape) == n_out
    return pl.pallas_call(
        functools.partial(_proj_kernel, has_branch=branch is not None, has_in_proj=w_in is not None),
        grid=(m // tm,),
        in_specs=specs,
        out_specs=out_specs,
        out_shape=out_shape,
        compiler_params=pltpu.CompilerParams(dimension_semantics=("arbitrary",), vmem_limit_bytes=VMEM_LIMIT),
        name="proj",
    )(*args)


def _prompt_attn_kernel(*refs, moba, n_blocks, group, scale, lam_init):
    if moba:
        slopes_ref, q_ref, k_ref, v_ref, o_ref, kb, vt, t_a, t_b, km, neg_scr = refs
    else:
        slopes_ref, lam_ref, subln_ref, q_ref, k_ref, v_ref, o_ref, kb, vt, t_a, t_b = refs
    blk = MOBA_BLOCK
    grp = pl.program_id(1)
    i = pl.program_id(2)
    half = LANES // 2

    @pl.when(i == 0)
    def _():
        pos = lax.broadcasted_iota(jnp.int32, (blk, LANES), 0)
        pos = jnp.where(lax.broadcasted_iota(jnp.int32, (blk, LANES), 1) < N_SLOPE_PARTS, pos, 0)
        key_pos = pos.astype(F32).astype(BF16)
        for n in range(n_blocks):
            kblk = k_ref[n * blk:(n + 1) * blk, :]
            kb[n, :, :LANES] = kblk.astype(BF16)
            kb[n, :, LANES:] = key_pos
            vt[n] = v_ref[n * blk:(n + 1) * blk, :].T.astype(BF16)
            if moba:
                km[n:n + 1, :] = jnp.sum(kblk, axis=0, keepdims=True) * (1.0 / blk)

    qt = q_ref[...].T
    ch = lax.broadcasted_iota(jnp.int32, (LANES, 1), 0)
    ch_masks = (ch < half, ch >= half)
    causal = (lax.broadcasted_iota(jnp.int32, (blk, blk), 0) <= lax.broadcasted_iota(jnp.int32, (blk, blk), 1))

    qts, slopes = [], []
    for e in range(2):
        slope = (slopes_ref[2 * grp + e] if moba else slopes_ref[grp]) * LOG2E
        qte = jnp.where(ch_masks[e], qt, 0.0)
        rest = jnp.full((LANES, blk), slope, F32)
        slope_rows = jnp.zeros((LANES, blk), F32)
        for part in range(N_SLOPE_PARTS):
            piece = rest.astype(BF16).astype(F32)
            slope_rows = jnp.where(ch == part, piece, slope_rows)
            rest = rest - piece
        qts.append(jnp.concatenate([(qte * (scale * LOG2E)).astype(BF16), slope_rows.astype(BF16)], axis=0))
        slopes.append(slope)
        if moba:
            gate = _dot_3pass(km[...], qte)
            past = lax.broadcasted_iota(jnp.int32, gate.shape, 0) < i
            sel = _top_k_mask(jnp.where(past, gate, NEG), min(MOBA_TOPK, n_blocks), axis=0)
            neg = (1.0 - sel) * NEG
            for n in range(n_blocks):
                neg_scr[e, n] = neg[n:n + 1, :]

    n_val = half if moba else LANES

    def v_rows(blocks, e):
        vals = [vt[n, e * half:(e + 1) * half, :] if moba else vt[n] for n in blocks]
        vals = vals[0] if len(vals) == 1 else jnp.concatenate(vals, axis=1)
        return jnp.concatenate([vals, jnp.ones((ONES_ROWS, vals.shape[1]), BF16)], axis=0)

    def store_scores(t_ref, g):
        n0 = jnp.minimum(g * group, n_blocks - group)
        kg = kb[pl.ds(n0, group)].reshape(group * blk, 2 * LANES)
        for e in range(2):
            t_ref[e] = _dot(kg, qts[e])

    def attend_group(t_ref, g, state):
        n0 = g * group
        new_state = []
        for e in range(2):
            cs, block_max = [], []
            for u in range(group):
                n = n0 + u
                c = slopes[e] * ((n - i) * blk).astype(F32)
                if moba:
                    c = c + neg_scr[e, jnp.minimum(n, n_blocks - 1)]
                c = jnp.where(n < i, c, NEG)
                cs.append(c)
                block_max.append(_reduce_rows(t_ref[e, u * blk:(u + 1) * blk, :], jnp.maximum) + c)
            m_old, acc_old = state[e]
            m_new = m_old
            for bm in block_max:
                m_new = jnp.maximum(m_new, bm)
            alpha = jnp.exp2(m_old - m_new)
            p_all = jnp.concatenate(
                [jnp.exp2(t_ref[e, u * blk:(u + 1) * blk, :] - (m_new - cs[u])).astype(BF16) for u in range(group)],
                axis=0)
            v_all = v_rows([jnp.minimum(n0 + u, n_blocks - 1) for u in range(group)], e)
            new_state.append((m_new, alpha * acc_old + _dot(v_all, p_all)))
        return tuple(new_state)

    store_scores(t_a, 0)

    state = []
    for e in range(2):
        t = jnp.where(causal, _dot(kb[i], qts[e]), NEG)
        m = _reduce_rows(t, jnp.maximum)
        state.append((m, _dot(v_rows([i], e), jnp.exp2(t - m).astype(BF16))))

    def attend_pair(j, state):
        store_scores(t_b, 2 * j + 1)
        state = attend_group(t_a, 2 * j, state)
        store_scores(t_a, 2 * j + 2)
        return attend_group(t_b, 2 * j + 1, state)

    state = lax.fori_loop(0, lax.div(i + (2 * group - 1), 2 * group), attend_pair, tuple(state))

    outs = [acc[:n_val] / acc[n_val:n_val + 1] for (_, acc) in state]
    if moba:
        o_ref[...] = jnp.concatenate(outs, axis=0).T
    else:
        o = (outs[0] - lam_ref[...] * outs[1]).T
        o_ref[...] = _rms(o, subln_ref[...]) * (1.0 - lam_init)


def _lambda_kernel(q1_ref, k1_ref, q2_ref, k2_ref, lam_ref, *, lam_init):
    a = jnp.sum(q1_ref[...] * k1_ref[...], axis=1, keepdims=True)
    b = jnp.sum(q2_ref[...] * k2_ref[...], axis=1, keepdims=True)
    lam_ref[...] = jnp.exp(a) - jnp.exp(b) + lam_init


def _lambda(lq1, lk1, lq2, lk2, lam_init):
    args = [v.reshape(1, -1) for v in (lq1, lk1, lq2, lk2)]
    return pl.pallas_call(
        functools.partial(_lambda_kernel, lam_init=lam_init),
        out_shape=jax.ShapeDtypeStruct((1, 1), F32),
        name="lambda",
    )(*args)


def _prompt_attn(q, k, v, slopes, *, batch, moba, lam=None, subln=None, lam_init=0.0):
    m, da = q.shape
    seq = m // batch
    blk = MOBA_BLOCK
    assert seq % blk == 0 and da % LANES == 0
    n_groups = da // LANES
    n_blocks = seq // blk
    q_spec = pl.BlockSpec((blk, LANES), lambda b, g, i: (b * n_blocks + i, g))
    kv_spec = pl.BlockSpec((seq, LANES), lambda b, g, i: (b, g))
    smem = pl.BlockSpec(memory_space=pltpu.SMEM)
    args, specs = [slopes], [smem]
    group = math.gcd(n_blocks, KEY_GROUP)
    scratch = [pltpu.VMEM((n_blocks, blk, 2 * LANES), BF16),
               pltpu.VMEM((n_blocks, LANES, blk), BF16),
               pltpu.VMEM((2, group * blk, blk), F32),
               pltpu.VMEM((2, group * blk, blk), F32)]
    if moba:
        scratch += [pltpu.VMEM((n_blocks, LANES), F32),
                    pltpu.VMEM((2, n_blocks, 1, blk), F32)]
    else:
        args += [lam, subln.reshape(1, LANES)]
        specs += [pl.BlockSpec((1, 1), lambda b, g, i: (0, 0)), pl.BlockSpec((1, LANES), lambda b, g, i: (0, 0))]
    args += [q, k, v]
    specs += [q_spec, kv_spec, kv_spec]
    return pl.pallas_call(
        functools.partial(_prompt_attn_kernel, moba=moba, n_blocks=n_blocks, group=group,
                          scale=(LANES // 2) ** -0.5, lam_init=lam_init),
        grid=(batch, n_groups, n_blocks),
        in_specs=specs,
        out_specs=q_spec,
        out_shape=jax.ShapeDtypeStruct((m, da), F32),
        scratch_shapes=scratch,
        compiler_params=pltpu.CompilerParams(dimension_semantics=("arbitrary",) * 3, vmem_limit_bytes=VMEM_LIMIT),
        name="moba_prompt" if moba else "diff_prompt",
    )(*args)


def _paged_attn_kernel(*refs, moba, v_token_minor, n_chunks, n_q, past_len, page, lam_init):
    pps = PAGES_PER_STEP
    pt_ref, slopes_ref = refs[:2]
    refs = refs[2:]
    if not moba:
        lam_ref, subln_ref = refs[:2]
        refs = refs[2:]
    q_ref, kn_ref, vn_ref = refs[:3]
    k_pages = refs[3:3 + pps]
    v_pages = refs[3 + pps:3 + 2 * pps]
    refs = refs[3 + 2 * pps:]
    o_ref, qbd_scr, s_scr, p_scr, pown_scr, acc_scr = refs[:6]
    if moba:
        kmean_scr = refs[6]
    del pt_ref
    step = pl.program_id(1)
    d = q_ref.shape[2]
    half = LANES // 2
    n_grp = d // half
    rows = n_q * n_grp
    chunk = pps * page
    pages_per_blk = MOBA_BLOCK // page
    grp_of_lane = lax.broadcasted_iota(jnp.int32, (n_grp, d), 1) // half
    grp_of_row = lax.broadcasted_iota(jnp.int32, (n_grp, d), 0)
    diag = grp_of_lane == grp_of_row

    @pl.when(step == 0)
    def _():
        q = q_ref[0]
        for t in range(n_q):
            qbd_scr[t * n_grp:(t + 1) * n_grp, :] = jnp.where(diag, jnp.broadcast_to(q[t:t + 1, :], (n_grp, d)), 0.0)
        if moba:
            kmean_scr[...] = jnp.zeros(kmean_scr.shape, F32)

    @pl.when(step < n_chunks)
    def _():
        pages = [ref[0, 0] for ref in k_pages]
        kc = jnp.concatenate([pg.astype(BF16) for pg in pages], axis=1)
        scale = half ** -0.5
        s_scr[step] = _dot((qbd_scr[...] * scale).astype(BF16), kc)
        if moba:
            blk_lane = lax.broadcasted_iota(jnp.int32, (1, LANES), 1)
            for a in range(pps // pages_per_blk):
                tot = pages[a * pages_per_blk]
                for b in range(1, pages_per_blk):
                    tot = tot + pages[a * pages_per_blk + b]
                mean = jnp.sum(tot, axis=1, keepdims=True) * (1.0 / MOBA_BLOCK)
                n = step * (pps // pages_per_blk) + a
                kmean_scr[...] = jnp.where(blk_lane == n, mean, kmean_scr[...])

    @pl.when(step == n_chunks - 1)
    def _():
        qbd = qbd_scr[...]
        row = lax.broadcasted_iota(jnp.int32, (rows, 1), 0)
        q_pos = past_len + row // n_grp
        slope = jnp.zeros((rows, 1), F32)
        for g in range(n_grp):
            s_g = slopes_ref[g] if moba else slopes_ref[g // 2]
            slope = jnp.where(row % n_grp == g, s_g, slope)
        key_in_chunk = lax.broadcasted_iota(jnp.int32, (rows, chunk), 1)
        if moba:
            n_blk = past_len // MOBA_BLOCK
            gate = _dot_3pass(qbd, kmean_scr[...])
            is_blk = lax.broadcasted_iota(jnp.int32, gate.shape, 1) < n_blk
            sel = _top_k_mask(jnp.where(is_blk, gate, NEG), min(MOBA_TOPK, n_blk)).astype(BF16)
            blk_per_chunk = chunk // MOBA_BLOCK
            e_row = lax.broadcasted_iota(jnp.int32, (LANES, chunk), 0)
            e_col = lax.broadcasted_iota(jnp.int32, (LANES, chunk), 1) // MOBA_BLOCK
        kn = kn_ref[0]
        s_own = []
        for t in range(n_q):
            s_t = jnp.sum(qbd * kn[t:t + 1, :], axis=1, keepdims=True) * (half ** -0.5)
            s_t = s_t - slope * (q_pos - (past_len + t)).astype(F32)
            s_own.append(jnp.where(q_pos >= past_len + t, s_t, NEG))
        m = s_own[0]
        for t in range(1, n_q):
            m = jnp.maximum(m, s_own[t])

        def biased(c):
            s = s_scr[c] - slope * (q_pos - (c * chunk + key_in_chunk)).astype(F32)
            if moba:
                expand = jnp.where(e_row == c * blk_per_chunk + e_col, 1.0, 0.0).astype(BF16)
                s = jnp.where(_dot(sel, expand) > 0.5, s, NEG)
            return s

        def max_body(c, m):
            s = biased(c)
            s_scr[c] = s
            return jnp.maximum(m, jnp.max(s, axis=1, keepdims=True))

        m = lax.fori_loop(0, n_chunks, max_body, m)

        def sum_body(c, l):
            p = jnp.exp(s_scr[c] - m)
            s_scr[c] = p
            return l + jnp.sum(p, axis=1, keepdims=True)

        p_own = [jnp.exp(s_t - m) for s_t in s_own]
        l = p_own[0]
        for t in range(1, n_q):
            l = l + p_own[t]
        l = lax.fori_loop(0, n_chunks, sum_body, l)
        inv = 1.0 / l

        def norm_body(c, carry):
            p_scr[c] = (s_scr[c] * inv).astype(BF16)
            return carry

        lax.fori_loop(0, n_chunks, norm_body, 0)
        for t in range(n_q):
            pown_scr[:, t:t + 1] = p_own[t] * inv

    @pl.when(step >= n_chunks)
    def _():
        if v_token_minor:
            vc = jnp.concatenate([ref[0, 0].astype(BF16) for ref in v_pages], axis=1)
            contrib = _dot_nt(p_scr[step - n_chunks], vc)
        else:
            n_head = d // LANES
            vc = jnp.concatenate(
                [jnp.concatenate([ref[0, 0, pl.ds(h, page, stride=n_head), :].astype(BF16) for h in range(n_head)],
                                 axis=1) for ref in v_pages], axis=0)
            contrib = _dot(p_scr[step - n_chunks], vc)

        @pl.when(step == n_chunks)
        def _():
            acc_scr[...] = contrib

        @pl.when(step > n_chunks)
        def _():
            acc_scr[...] += contrib

    @pl.when(step == 2 * n_chunks - 1)
    def _():
        vn = vn_ref[0]
        acc = acc_scr[...]
        for t in range(n_q):
            acc = acc + pown_scr[:, t:t + 1] * vn[t:t + 1, :]
        if moba:
            for t in range(n_q):
                o_ref[0, t:t + 1, :] = jnp.sum(jnp.where(diag, acc[t * n_grp:(t + 1) * n_grp, :], 0.0),
                                               axis=0, keepdims=True)
        else:
            lane_head = lax.broadcasted_iota(jnp.int32, (n_grp, d), 1) // LANES
            row_grp = lax.broadcasted_iota(jnp.int32, (n_grp, d), 0)
            coef = jnp.where(row_grp % 2 == 0, 1.0, -lam_ref[...])
            coef = jnp.where(lane_head == row_grp // 2, coef, 0.0)
            subln = subln_ref[...]
            for t in range(n_q):
                o_t = jnp.sum(acc[t * n_grp:(t + 1) * n_grp, :] * coef, axis=0, keepdims=True)
                for h in range(d // LANES):
                    seg = o_t[:, h * LANES:(h + 1) * LANES]
                    o_ref[0, t:t + 1, h * LANES:(h + 1) * LANES] = _rms(seg, subln) * (1.0 - lam_init)


def _paged_attn(q, k_new, v_new, cache_k, cache_v, layer, page_table, slopes, *, moba, v_token_minor, lam=None,
                subln=None, lam_init=0.0):
    bd, n_q, d = q.shape
    page = cache_k.shape[3]
    n_pages = page_table.shape[1]
    past_len = n_pages * page
    pps = PAGES_PER_STEP
    assert n_pages % pps == 0 and MOBA_BLOCK % page == 0 and pps % (MOBA_BLOCK // page) == 0
    assert past_len % MOBA_BLOCK == 0 and n_q <= MOBA_BLOCK
    n_chunks = n_pages // pps
    chunk = pps * page
    half = LANES // 2
    n_grp = d // half
    rows = n_q * n_grp

    tok_spec = pl.BlockSpec((1, n_q, d), lambda b, s, pt: (b, 0, 0))

    def k_page_spec(p):
        return pl.BlockSpec((1, 1, d, page),
                            lambda b, s, pt: (layer, pt[b, jnp.minimum(s, n_chunks - 1) * pps + p], 0, 0))

    def v_page_spec(p):
        return pl.BlockSpec((1, 1, d, page) if v_token_minor else (1, 1, page * (d // LANES), LANES),
                            lambda b, s, pt: (layer, pt[b, jnp.maximum(s - n_chunks, 0) * pps + p], 0, 0))

    smem = pl.BlockSpec(memory_space=pltpu.SMEM)
    args, specs = [slopes], [smem]
    if not moba:
        args += [lam, subln.reshape(1, LANES)]
        specs += [pl.BlockSpec((1, 1), lambda b, s, pt: (0, 0)), pl.BlockSpec((1, LANES), lambda b, s, pt: (0, 0))]
    args += [q, k_new, v_new] + [cache_k] * pps + [cache_v] * pps
    specs += [tok_spec] * 3 + [k_page_spec(p) for p in range(pps)] + [v_page_spec(p) for p in range(pps)]
    scratch = [
        pltpu.VMEM((rows, d), F32),
        pltpu.VMEM((n_chunks, rows, chunk), F32),
        pltpu.VMEM((n_chunks, rows, chunk), BF16),
        pltpu.VMEM((rows, LANES), F32),
        pltpu.VMEM((rows, d), F32),
    ]
    if moba:
        assert past_len // MOBA_BLOCK <= LANES
        scratch.append(pltpu.VMEM((d, LANES), F32))
    grid_spec = pltpu.PrefetchScalarGridSpec(
        num_scalar_prefetch=1,
        grid=(bd, 2 * n_chunks),
        in_specs=specs,
        out_specs=tok_spec,
        scratch_shapes=scratch,
    )
    return pl.pallas_call(
        functools.partial(_paged_attn_kernel, moba=moba, v_token_minor=v_token_minor, n_chunks=n_chunks, n_q=n_q,
                          past_len=past_len, page=page, lam_init=lam_init),
        grid_spec=grid_spec,
        out_shape=jax.ShapeDtypeStruct((bd, n_q, d), F32),
        compiler_params=pltpu.CompilerParams(dimension_semantics=("arbitrary",) * 2, vmem_limit_bytes=VMEM_LIMIT),
        name="moba_sample" if moba else "diff_sample",
    )(page_table, *args)


def _trunk(x, page_table, caches, weights):
    (norm_a, w_in_a, w_out_a, norm_b, w_in_b, w_out_b, lq1, lk1, lq2, lk2, subln_b, final_norm) = weights
    cache_k_moba, cache_v_moba, cache_k_diff, cache_v_diff = caches
    bsz, t, d = x.shape
    depth = w_in_a.shape[0] + w_in_b.shape[0]
    h_a = d // (LANES // 2)
    h_b = d // LANES
    slopes_a = _alibi_slopes(h_a)
    slopes_b = _alibi_slopes(h_b)
    xf = x.reshape(bsz * t, d)
    ka, va, kd, vd = [], [], [], []
    branch = None
    for i in range(depth):
        j = i // 2
        moba = i % 2 == 0
        norm, w_in, w_out = (norm_a[j], w_in_a[j], w_out_a[j]) if moba else (norm_b[j], w_in_b[j], w_out_b[j])
        outs = _proj(xf, norm, branch=branch, w_in=w_in)
        if branch is not None:
            xf = outs[0]
            outs = outs[1:]
        q, k, v, z = outs
        lam_init = _lambda_init(i)
        kwargs = {}
        if not moba:
            kwargs = dict(lam=_lambda(lq1[j], lk1[j], lq2[j], lk2[j], lam_init), subln=subln_b[j], lam_init=lam_init)
        slopes = slopes_a if moba else slopes_b
        if page_table is None:
            o = _prompt_attn(q, k, v, slopes, batch=bsz, moba=moba, **kwargs)
        else:
            ck, cv = (cache_k_moba, cache_v_moba) if moba else (cache_k_diff, cache_v_diff)
            n_pool, page = ck.shape[1], ck.shape[2]
            v_token_minor = cv.shape[-1] < LANES
            ck = jnp.swapaxes(ck.reshape(ck.shape[0], n_pool, page, d), 2, 3)
            if v_token_minor:
                cv = jnp.swapaxes(cv.reshape(cv.shape[0], n_pool, page, d), 2, 3)
            else:
                assert cv.shape[-1] == LANES
                cv = cv.reshape(cv.shape[0], n_pool, page * (d // LANES), LANES)
            o = _paged_attn(q.reshape(bsz, t, d), k.reshape(bsz, t, d), v.reshape(bsz, t, d), ck, cv,
                            j, page_table, slopes, moba=moba, v_token_minor=v_token_minor,
                            **kwargs).reshape(bsz * t, d)
        if moba:
            ka.append(k.reshape(bsz, t, h_a, d // h_a))
            va.append(v.reshape(bsz, t, h_a, d // h_a))
        else:
            kd.append(k.reshape(bsz, t, h_b, 2, d // (2 * h_b)))
            vd.append(v.reshape(bsz, t, h_b, d // h_b))
        branch = (o, z, w_out)
    y = _proj(xf, final_norm, branch=branch).reshape(bsz, t, d)
    return y, jnp.stack(ka), jnp.stack(va), jnp.stack(kd), jnp.stack(vd)


def kernel(x_prompt, x_sample, cache_k_moba, cache_v_moba, cache_k_diff, cache_v_diff, page_table, norm_a, w_in_a, w_out_a, norm_b, w_in_b, w_out_b, lambda_q1, lambda_k1, lambda_q2, lambda_k2, subln_b, final_norm):
    weights = (norm_a, w_in_a, w_out_a, norm_b, w_in_b, w_out_b, lambda_q1, lambda_k1, lambda_q2, lambda_k2,
               subln_b, final_norm)
    caches = (cache_k_moba, cache_v_moba, cache_k_diff, cache_v_diff)
    y_p, ka_p, va_p, kd_p, vd_p = _trunk(x_prompt, None, caches, weights)
    y_s, ka_s, va_s, kd_s, vd_s = _trunk(x_sample, page_table, caches, weights)
    return (y_p, y_s, ka_p, va_p, kd_p, vd_p, ka_s, va_s, kd_s, vd_s)
```

```python
import functools
import math

import jax
import jax.numpy as jnp
from jax import lax
from jax.experimental import pallas as pl
from jax.experimental.pallas import tpu as pltpu

MOBA_BLOCK = 256
MOBA_TOPK = 3
RMS_EPS = 1e-6
NEG = -1e30
LOG2E = math.log2(math.e)

LANES = 128
ROW_TILE = 256
KEY_GROUP = 2
N_SLOPE_PARTS = 3
ONES_ROWS = 16
PAGES_PER_STEP = 16
VMEM_LIMIT = 56 * 1024 * 1024

F32 = jnp.float32
BF16 = jnp.bfloat16


def _dot(a, b):
    return jnp.dot(a, b, preferred_element_type=F32)


def _dot_nt(a, b):
    return lax.dot_general(a, b, (((1,), (1,)), ((), ())), preferred_element_type=F32)


def _split(x):
    hi = x.astype(BF16)
    lo = (x - hi.astype(F32)).astype(BF16)
    return hi, lo


def _dot_3pass(a, b, dot=_dot):
    a_hi, a_lo = _split(a)
    b_hi, b_lo = _split(b)
    return dot(a_hi, b_hi) + (dot(a_hi, b_lo) + dot(a_lo, b_hi))


def _rms(x, g):
    return x * lax.rsqrt(jnp.mean(x * x, axis=-1, keepdims=True) + RMS_EPS) * g


def _alibi_slopes(n_heads):
    return 2.0 ** (-8.0 * jnp.arange(1, n_heads + 1, dtype=F32) / n_heads)


def _lambda_init(layer_idx):
    return 0.8 - 0.6 * math.exp(-0.3 * layer_idx)


def _reduce_rows(x, op):
    rows = x.shape[0]
    while rows > 8 and rows % 2 == 0:
        rows //= 2
        x = op(x[:rows], x[rows:])
    reduce = jnp.max if op is jnp.maximum else jnp.sum
    return reduce(x, axis=0, keepdims=True)


def _top_k_mask(gate, k, axis=1):
    col = lax.broadcasted_iota(jnp.int32, gate.shape, axis).astype(F32)
    sel = jnp.zeros(gate.shape, F32)
    for _ in range(k):
        mx = jnp.max(gate, axis=axis, keepdims=True)
        first = jnp.min(jnp.where(gate == mx, col, float(gate.shape[axis])), axis=axis, keepdims=True)
        hit = col == first
        sel = jnp.where(hit, jnp.where(mx > 0.5 * NEG, 1.0, 0.0), sel)
        gate = jnp.where(hit, -jnp.inf, gate)
    return sel


def _proj_kernel(*refs, has_branch, kinds):
    refs = list(refs)
    if has_branch:
        o_ref, z_ref, x_ref, w_out_ref = refs[:4]
        refs = refs[4:]
        z = z_ref[...]
        u = o_ref[...] * (z / (1.0 + jnp.exp(-z)))
        x = x_ref[...] + _dot(u.astype(BF16), w_out_ref[...])
    else:
        x = refs[0][...]
        refs = refs[1:]
    g_ref = refs[0]
    h = _rms(x, g_ref[...])
    if not kinds:
        refs[1][...] = h
        return
    w_refs = refs[1:1 + len(kinds)]
    outs = refs[1 + len(kinds):]
    if has_branch:
        outs[0][...] = x
        outs = outs[1:]
    hb = h.astype(BF16)
    tm = hb.shape[0]
    for kind, w_ref, out in zip(kinds, w_refs, outs):
        if kind == "rows":
            out[...] = _dot(hb, w_ref[...])
        elif kind == "cols":
            out[0] = _dot_nt(w_ref[...], hb)
        else:
            val = _dot(hb, w_ref[...])
            n_head = val.shape[1] // LANES
            for hd in range(n_head):
                out[pl.ds(hd, tm, stride=n_head), :] = val[:, hd * LANES:(hd + 1) * LANES]


def _proj(x, g, *, branch=None, w_in=None, outputs=((0, "rows"), (1, "rows"), (2, "rows"), (3, "rows")), batch=1):
    m, d = x.shape
    tm = min(ROW_TILE, m)
    assert m % tm == 0
    row = lambda i: (i, 0)
    fixed = lambda i: (0, 0)
    args, specs = [], []
    if branch is not None:
        o, z, w_out = branch
        args += [o, z, x, w_out.astype(BF16)]
        specs += [pl.BlockSpec((tm, o.shape[1]), row), pl.BlockSpec((tm, o.shape[1]), row),
                  pl.BlockSpec((tm, d), row), pl.BlockSpec(w_out.shape, fixed)]
    else:
        args.append(x)
        specs.append(pl.BlockSpec((tm, d), row))
    args.append(g.reshape(1, d))
    specs.append(pl.BlockSpec((1, d), fixed))
    kinds = ()
    if w_in is None:
        out_shape = jax.ShapeDtypeStruct((m, d), F32)
        out_specs = pl.BlockSpec((tm, d), row)
    else:
        kinds = tuple(kind for _, kind in outputs)
        da = w_in.shape[1] // 4
        seq = m // batch
        tiles_per_seq = max(seq // tm, 1)
        assert "cols" not in kinds or seq % tm == 0
        out_shape = [jax.ShapeDtypeStruct((m, d), F32)] if branch is not None else []
        out_specs = [pl.BlockSpec((tm, d), row)] if branch is not None else []
        for j, kind in outputs:
            w = w_in[:, j * da:(j + 1) * da].astype(BF16)
            args.append(w.T if kind == "cols" else w)
            specs.append(pl.BlockSpec(args[-1].shape, fixed))
            if kind == "rows":
                out_shape.append(jax.ShapeDtypeStruct((m, da), F32))
                out_specs.append(pl.BlockSpec((tm, da), row))
            elif kind == "cols":
                out_shape.append(jax.ShapeDtypeStruct((batch, da, seq), F32))
                out_specs.append(pl.BlockSpec((1, da, tm), lambda i: (i // tiles_per_seq, 0, i % tiles_per_seq)))
            else:
                assert kind == "tiles" and da % LANES == 0
                out_shape.append(jax.ShapeDtypeStruct((m * (da // LANES), LANES), F32))
                out_specs.append(pl.BlockSpec((tm * (da // LANES), LANES), row))
        out_shape, out_specs = tuple(out_shape), tuple(out_specs)
    return pl.pallas_call(
        functools.partial(_proj_kernel, has_branch=branch is not None, kinds=kinds),
        grid=(m // tm,),
        in_specs=specs,
        out_specs=out_specs,
        out_shape=out_shape,
        compiler_params=pltpu.CompilerParams(dimension_semantics=("arbitrary",), vmem_limit_bytes=VMEM_LIMIT),
        name="proj",
    )(*args)


def _prompt_attn_kernel(*refs, moba, n_blocks, group, scale, lam_init):
    if moba:
        slopes_ref, q_ref, k_ref, v_ref, o_ref, kb, vt, t_a, t_b, km, neg_scr = refs
    else:
        slopes_ref, lam_ref, subln_ref, q_ref, k_ref, v_ref, o_ref, kb, vt, t_a, t_b = refs
    blk = MOBA_BLOCK
    grp = pl.program_id(1)
    s = pl.program_id(2)
    half = LANES // 2

    @pl.when(s == 0)
    def _():
        pos = lax.broadcasted_iota(jnp.int32, (blk, LANES), 0)
        pos = jnp.where(lax.broadcasted_iota(jnp.int32, (blk, LANES), 1) < N_SLOPE_PARTS, pos, 0)
        key_pos = pos.astype(F32).astype(BF16)
        for n in range(n_blocks):
            kblk = k_ref[0, :, n * blk:(n + 1) * blk].T
            kb[n, :, :LANES] = kblk.astype(BF16)
            kb[n, :, LANES:] = key_pos
            vt[n] = v_ref[0, :, n * blk:(n + 1) * blk].astype(BF16)
            if moba:
                km[n:n + 1, :] = jnp.sum(kblk, axis=0, keepdims=True) * (1.0 / blk)

    tq = group * blk
    qt = q_ref[...].T
    ch = lax.broadcasted_iota(jnp.int32, (LANES, 1), 0)
    ch_masks = (ch < half, ch >= half)
    q_blk = lax.broadcasted_iota(jnp.int32, (1, tq), 1) // blk

    qts, slopes = [], []
    for e in range(2):
        slope = (slopes_ref[2 * grp + e] if moba else slopes_ref[grp]) * LOG2E
        qte = jnp.where(ch_masks[e], qt, 0.0)
        rest = jnp.full((LANES, tq), slope, F32)
        slope_rows = jnp.zeros((LANES, tq), F32)
        for part in range(N_SLOPE_PARTS):
            piece = rest.astype(BF16).astype(F32)
            slope_rows = jnp.where(ch == part, piece, slope_rows)
            rest = rest - piece
        qts.append(jnp.concatenate([(qte * (scale * LOG2E)).astype(BF16), slope_rows.astype(BF16)], axis=0))
        slopes.append(slope)
        if moba:
            gate = _dot_3pass(km[...], qte)
            past = lax.broadcasted_iota(jnp.int32, gate.shape, 0) < s * group + q_blk
            sel = _top_k_mask(jnp.where(past, gate, NEG), min(MOBA_TOPK, n_blocks), axis=0)
            neg = (1.0 - sel) * NEG
            for n in range(n_blocks):
                neg_scr[e, n] = neg[n:n + 1, :]

    n_val = half if moba else LANES

    def first_block(g):
        return jnp.minimum(g * group, n_blocks - group)

    def v_rows(g, e):
        n0 = first_block(g)
        vals = [vt[n0 + u, e * half:(e + 1) * half, :] if moba else vt[n0 + u] for u in range(group)]
        return jnp.concatenate([jnp.concatenate(vals, axis=1), jnp.ones((ONES_ROWS, tq), BF16)], axis=0)

    def store_scores(t_ref, g):
        kg = kb[pl.ds(first_block(g), group)].reshape(group * blk, 2 * LANES)
        for e in range(2):
            t_ref[e] = _dot(kg, qts[e])

    def attend_group(t_ref, g, state):
        new_state = []
        for e in range(2):
            cs, block_max = [], []
            for u in range(group):
                c = slopes[e] * (((g - s) * group + u) * blk).astype(F32)
                if state is not None:
                    if moba:
                        c = c + neg_scr[e, first_block(g) + u]
                    c = jnp.where(g < s, c, NEG)
                cs.append(c)
                block_max.append(_reduce_rows(t_ref[e, u * blk:(u + 1) * blk, :], jnp.maximum) + c)
            m_new = block_max[0] if state is None else state[e][0]
            for bm in block_max[1 if state is None else 0:]:
                m_new = jnp.maximum(m_new, bm)
            p_all = jnp.concatenate(
                [jnp.exp2(t_ref[e, u * blk:(u + 1) * blk, :] - (m_new - cs[u])).astype(BF16) for u in range(group)],
                axis=0)
            acc = _dot(v_rows(g, e), p_all)
            if state is not None:
                acc = jnp.exp2(state[e][0] - m_new) * state[e][1] + acc
            new_state.append((m_new, acc))
        return tuple(new_state)

    store_scores(t_a, s)
    store_scores(t_b, 0)

    key = lax.broadcasted_iota(jnp.int32, (blk, tq), 0)
    qry = lax.broadcasted_iota(jnp.int32, (blk, tq), 1)
    for e in range(2):
        for u in range(group):
            other = neg_scr[e, s * group + u] if moba else jnp.where(q_blk > u, 0.0, NEG)
            mask = jnp.where(q_blk == u, jnp.where(key + u * blk <= qry, 0.0, NEG), other)
            t_a[e, u * blk:(u + 1) * blk, :] = t_a[e, u * blk:(u + 1) * blk, :] + mask
    state = attend_group(t_a, s, None)

    def attend_pair(j, state):
        store_scores(t_a, 2 * j + 1)
        state = attend_group(t_b, 2 * j, state)
        store_scores(t_b, 2 * j + 2)
        return attend_group(t_a, 2 * j + 1, state)

    state = lax.fori_loop(0, lax.div(s + 1, 2), attend_pair, state)

    outs = [acc[:n_val] / acc[n_val:n_val + 1] for (_, acc) in state]
    if moba:
        o_ref[...] = jnp.concatenate(outs, axis=0).T
    else:
        o = (outs[0] - lam_ref[...] * outs[1]).T
        o_ref[...] = _rms(o, subln_ref[...]) * (1.0 - lam_init)


def _lambda_kernel(q1_ref, k1_ref, q2_ref, k2_ref, lam_ref, *, lam_init):
    a = jnp.sum(q1_ref[...] * k1_ref[...], axis=1, keepdims=True)
    b = jnp.sum(q2_ref[...] * k2_ref[...], axis=1, keepdims=True)
    lam_ref[...] = jnp.exp(a) - jnp.exp(b) + lam_init


def _lambda(lq1, lk1, lq2, lk2, lam_init):
    args = [v.reshape(1, -1) for v in (lq1, lk1, lq2, lk2)]
    return pl.pallas_call(
        functools.partial(_lambda_kernel, lam_init=lam_init),
        out_shape=jax.ShapeDtypeStruct((1, 1), F32),
        name="lambda",
    )(*args)


def _prompt_attn(q, k, v, slopes, *, batch, moba, lam=None, subln=None, lam_init=0.0):
    m, da = q.shape
    seq = m // batch
    blk = MOBA_BLOCK
    assert seq % blk == 0 and da % LANES == 0 and k.shape == v.shape == (batch, da, seq)
    n_groups = da // LANES
    n_blocks = seq // blk
    group = math.gcd(n_blocks, KEY_GROUP)
    tq = group * blk
    n_tiles = seq // tq
    q_spec = pl.BlockSpec((tq, LANES), lambda b, g, s: (b * n_tiles + s, g))
    kv_spec = pl.BlockSpec((1, LANES, seq), lambda b, g, s: (b, g, 0))
    smem = pl.BlockSpec(memory_space=pltpu.SMEM)
    args, specs = [slopes], [smem]
    scratch = [pltpu.VMEM((n_blocks, blk, 2 * LANES), BF16),
               pltpu.VMEM((n_blocks, LANES, blk), BF16),
               pltpu.VMEM((2, tq, tq), F32),
               pltpu.VMEM((2, tq, tq), F32)]
    if moba:
        scratch += [pltpu.VMEM((n_blocks, LANES), F32),
                    pltpu.VMEM((2, n_blocks, 1, tq), F32)]
    else:
        args += [lam, subln.reshape(1, LANES)]
        specs += [pl.BlockSpec((1, 1), lambda b, g, i: (0, 0)), pl.BlockSpec((1, LANES), lambda b, g, i: (0, 0))]
    args += [q, k, v]
    specs += [q_spec, kv_spec, kv_spec]
    return pl.pallas_call(
        functools.partial(_prompt_attn_kernel, moba=moba, n_blocks=n_blocks, group=group,
                          scale=(LANES // 2) ** -0.5, lam_init=lam_init),
        grid=(batch, n_groups, n_tiles),
        in_specs=specs,
        out_specs=q_spec,
        out_shape=jax.ShapeDtypeStruct((m, da), F32),
        scratch_shapes=scratch,
        compiler_params=pltpu.CompilerParams(dimension_semantics=("arbitrary",) * 3, vmem_limit_bytes=VMEM_LIMIT),
        name="moba_prompt" if moba else "diff_prompt",
    )(*args)


def _paged_attn_kernel(*refs, moba, v_token_minor, n_chunks, n_q, past_len, page, lam_init):
    pps = PAGES_PER_STEP
    pt_ref, slopes_ref = refs[:2]
    refs = refs[2:]
    if not moba:
        lam_ref, subln_ref = refs[:2]
        refs = refs[2:]
    q_ref, kn_ref, vn_ref = refs[:3]
    k_pages = refs[3:3 + pps]
    v_pages = refs[3 + pps:3 + 2 * pps]
    refs = refs[3 + 2 * pps:]
    o_ref, qbd_scr, s_scr, p_scr, pown_scr, acc_scr = refs[:6]
    if moba:
        kmean_scr = refs[6]
    del pt_ref
    step = pl.program_id(1)
    d = q_ref.shape[2]
    half = LANES // 2
    n_grp = d // half
    rows = n_q * n_grp
    chunk = pps * page
    pages_per_blk = MOBA_BLOCK // page
    grp_of_lane = lax.broadcasted_iota(jnp.int32, (n_grp, d), 1) // half
    grp_of_row = lax.broadcasted_iota(jnp.int32, (n_grp, d), 0)
    diag = grp_of_lane == grp_of_row

    @pl.when(step == 0)
    def _():
        q = q_ref[0]
        for t in range(n_q):
            qbd_scr[t * n_grp:(t + 1) * n_grp, :] = jnp.where(diag, jnp.broadcast_to(q[t:t + 1, :], (n_grp, d)), 0.0)
        if moba:
            kmean_scr[...] = jnp.zeros(kmean_scr.shape, F32)

    @pl.when(step < n_chunks)
    def _():
        pages = [ref[0, 0] for ref in k_pages]
        kc = jnp.concatenate([pg.astype(BF16) for pg in pages], axis=1)
        scale = half ** -0.5
        s_scr[step] = _dot((qbd_scr[...] * scale).astype(BF16), kc)
        if moba:
            blk_lane = lax.broadcasted_iota(jnp.int32, (1, LANES), 1)
            for a in range(pps // pages_per_blk):
                tot = pages[a * pages_per_blk]
                for b in range(1, pages_per_blk):
                    tot = tot + pages[a * pages_per_blk + b]
                mean = jnp.sum(tot, axis=1, keepdims=True) * (1.0 / MOBA_BLOCK)
                n = step * (pps // pages_per_blk) + a
                kmean_scr[...] = jnp.where(blk_lane == n, mean, kmean_scr[...])

    @pl.when(step == n_chunks - 1)
    def _():
        qbd = qbd_scr[...]
        row = lax.broadcasted_iota(jnp.int32, (rows, 1), 0)
        q_pos = past_len + row // n_grp
        slope = jnp.zeros((rows, 1), F32)
        for g in range(n_grp):
            s_g = slopes_ref[g] if moba else slopes_ref[g // 2]
            slope = jnp.where(row % n_grp == g, s_g, slope)
        key_in_chunk = lax.broadcasted_iota(jnp.int32, (rows, chunk), 1)
        if moba:
            n_blk = past_len // MOBA_BLOCK
            gate = _dot_3pass(qbd, kmean_scr[...])
            is_blk = lax.broadcasted_iota(jnp.int32, gate.shape, 1) < n_blk
            sel = _top_k_mask(jnp.where(is_blk, gate, NEG), min(MOBA_TOPK, n_blk)).astype(BF16)
            blk_per_chunk = chunk // MOBA_BLOCK
            e_row = lax.broadcasted_iota(jnp.int32, (LANES, chunk), 0)
            e_col = lax.broadcasted_iota(jnp.int32, (LANES, chunk), 1) // MOBA_BLOCK
        kn = kn_ref[0]
        s_own = []
        for t in range(n_q):
            s_t = jnp.sum(qbd * kn[t:t + 1, :], axis=1, keepdims=True) * (half ** -0.5)
            s_t = s_t - slope * (q_pos - (past_len + t)).astype(F32)
            s_own.append(jnp.where(q_pos >= past_len + t, s_t, NEG))
        m = s_own[0]
        for t in range(1, n_q):
            m = jnp.maximum(m, s_own[t])

        def biased(c):
            s = s_scr[c] - slope * (q_pos - (c * chunk + key_in_chunk)).astype(F32)
            if moba:
                expand = jnp.where(e_row == c * blk_per_chunk + e_col, 1.0, 0.0).astype(BF16)
                s = jnp.where(_dot(sel, expand) > 0.5, s, NEG)
            return s

        def max_body(c, m):
            s = biased(c)
            s_scr[c] = s
            return jnp.maximum(m, jnp.max(s, axis=1, keepdims=True))

        m = lax.fori_loop(0, n_chunks, max_body, m)

        def sum_body(c, l):
            p = jnp.exp(s_scr[c] - m)
            s_scr[c] = p
            return l + jnp.sum(p, axis=1, keepdims=True)

        p_own = [jnp.exp(s_t - m) for s_t in s_own]
        l = p_own[0]
        for t in range(1, n_q):
            l = l + p_own[t]
        l = lax.fori_loop(0, n_chunks, sum_body, l)
        inv = 1.0 / l

        def norm_body(c, carry):
            p_scr[c] = (s_scr[c] * inv).astype(BF16)
            return carry

        lax.fori_loop(0, n_chunks, norm_body, 0)
        for t in range(n_q):
            pown_scr[:, t:t + 1] = p_own[t] * inv

    @pl.when(step >= n_chunks)
    def _():
        if v_token_minor:
            vc = jnp.concatenate([ref[0, 0].astype(BF16) for ref in v_pages], axis=1)
            contrib = _dot_nt(p_scr[step - n_chunks], vc)
        else:
            n_head = d // LANES
            vc = jnp.concatenate(
                [jnp.concatenate([ref[0, 0, pl.ds(h, page, stride=n_head), :].astype(BF16) for h in range(n_head)],
                                 axis=1) for ref in v_pages], axis=0)
            contrib = _dot(p_scr[step - n_chunks], vc)

        @pl.when(step == n_chunks)
        def _():
            acc_scr[...] = contrib

        @pl.when(step > n_chunks)
        def _():
            acc_scr[...] += contrib

    @pl.when(step == 2 * n_chunks - 1)
    def _():
        vn = vn_ref[0]
        acc = acc_scr[...]
        for t in range(n_q):
            acc = acc + pown_scr[:, t:t + 1] * vn[t:t + 1, :]
        if moba:
            for t in range(n_q):
                o_ref[0, t:t + 1, :] = jnp.sum(jnp.where(diag, acc[t * n_grp:(t + 1) * n_grp, :], 0.0),
                                               axis=0, keepdims=True)
        else:
            lane_head = lax.broadcasted_iota(jnp.int32, (n_grp, d), 1) // LANES
            row_grp = lax.broadcasted_iota(jnp.int32, (n_grp, d), 0)
            coef = jnp.where(row_grp % 2 == 0, 1.0, -lam_ref[...])
            coef = jnp.where(lane_head == row_grp // 2, coef, 0.0)
            subln = subln_ref[...]
            for t in range(n_q):
                o_t = jnp.sum(acc[t * n_grp:(t + 1) * n_grp, :] * coef, axis=0, keepdims=True)
                for h in range(d // LANES):
                    seg = o_t[:, h * LANES:(h + 1) * LANES]
                    o_ref[0, t:t + 1, h * LANES:(h + 1) * LANES] = _rms(seg, subln) * (1.0 - lam_init)


def _paged_attn(q, k_new, v_new, cache_k, cache_v, layer, page_table, slopes, *, moba, v_token_minor, lam=None,
                subln=None, lam_init=0.0):
    bd, n_q, d = q.shape
    page = cache_k.shape[3]
    n_pages = page_table.shape[1]
    past_len = n_pages * page
    pps = PAGES_PER_STEP
    assert n_pages % pps == 0 and MOBA_BLOCK % page == 0 and pps % (MOBA_BLOCK // page) == 0
    assert past_len % MOBA_BLOCK == 0 and n_q <= MOBA_BLOCK
    n_chunks = n_pages // pps
    chunk = pps * page
    half = LANES // 2
    n_grp = d // half
    rows = n_q * n_grp

    tok_spec = pl.BlockSpec((1, n_q, d), lambda b, s, pt: (b, 0, 0))

    def k_page_spec(p):
        return pl.BlockSpec((1, 1, d, page),
                            lambda b, s, pt: (layer, pt[b, jnp.minimum(s, n_chunks - 1) * pps + p], 0, 0))

    def v_page_spec(p):
        return pl.BlockSpec((1, 1, d, page) if v_token_minor else (1, 1, page * (d // LANES), LANES),
                            lambda b, s, pt: (layer, pt[b, jnp.maximum(s - n_chunks, 0) * pps + p], 0, 0))

    smem = pl.BlockSpec(memory_space=pltpu.SMEM)
    args, specs = [slopes], [smem]
    if not moba:
        args += [lam, subln.reshape(1, LANES)]
        specs += [pl.BlockSpec((1, 1), lambda b, s, pt: (0, 0)), pl.BlockSpec((1, LANES), lambda b, s, pt: (0, 0))]
    args += [q, k_new, v_new] + [cache_k] * pps + [cache_v] * pps
    specs += [tok_spec] * 3 + [k_page_spec(p) for p in range(pps)] + [v_page_spec(p) for p in range(pps)]
    scratch = [
        pltpu.VMEM((rows, d), F32),
        pltpu.VMEM((n_chunks, rows, chunk), F32),
        pltpu.VMEM((n_chunks, rows, chunk), BF16),
        pltpu.VMEM((rows, LANES), F32),
        pltpu.VMEM((rows, d), F32),
    ]
    if moba:
        assert past_len // MOBA_BLOCK <= LANES
        scratch.append(pltpu.VMEM((d, LANES), F32))
    grid_spec = pltpu.PrefetchScalarGridSpec(
        num_scalar_prefetch=1,
        grid=(bd, 2 * n_chunks),
        in_specs=specs,
        out_specs=tok_spec,
        scratch_shapes=scratch,
    )
    return pl.pallas_call(
        functools.partial(_paged_attn_kernel, moba=moba, v_token_minor=v_token_minor, n_chunks=n_chunks, n_q=n_q,
                          past_len=past_len, page=page, lam_init=lam_init),
        grid_spec=grid_spec,
        out_shape=jax.ShapeDtypeStruct((bd, n_q, d), F32),
        compiler_params=pltpu.CompilerParams(dimension_semantics=("arbitrary",) * 2, vmem_limit_bytes=VMEM_LIMIT),
        name="moba_sample" if moba else "diff_sample",
    )(page_table, *args)


def _trunk(x, page_table, caches, weights):
    (norm_a, w_in_a, w_out_a, norm_b, w_in_b, w_out_b, lq1, lk1, lq2, lk2, subln_b, final_norm) = weights
    cache_k_moba, cache_v_moba, cache_k_diff, cache_v_diff = caches
    bsz, t, d = x.shape
    depth = w_in_a.shape[0] + w_in_b.shape[0]
    h_a = d // (LANES // 2)
    h_b = d // LANES
    slopes_a = _alibi_slopes(h_a)
    slopes_b = _alibi_slopes(h_b)
    xf = x.reshape(bsz * t, d)
    ka, va, kd, vd = [], [], [], []
    branch = None
    for i in range(depth):
        j = i // 2
        moba = i % 2 == 0
        norm, w_in, w_out = (norm_a[j], w_in_a[j], w_out_a[j]) if moba else (norm_b[j], w_in_b[j], w_out_b[j])
        prompt = page_table is None
        dh_a, dh_b = d // h_a, d // (2 * h_b)
        if prompt:
            outputs = ((0, "rows"), (1, "cols"), (2, "cols"), (3, "rows")) if moba else \
                      ((0, "rows"), (1, "cols"), (2, "tiles"), (2, "cols"), (3, "rows"))
        else:
            outputs = ((0, "rows"), (1, "rows"), (2, "rows"), (3, "rows"))
        outs = _proj(xf, norm, branch=branch, w_in=w_in, outputs=outputs, batch=bsz)
        if branch is not None:
            xf = outs[0]
            outs = outs[1:]
        q, k, z = outs[0], outs[1], outs[-1]
        v = outs[-2]
        lam_init = _lambda_init(i)
        kwargs = {}
        if not moba:
            kwargs = dict(lam=_lambda(lq1[j], lk1[j], lq2[j], lk2[j], lam_init), subln=subln_b[j], lam_init=lam_init)
        slopes = slopes_a if moba else slopes_b
        if prompt:
            o = _prompt_attn(q, k, v, slopes, batch=bsz, moba=moba, **kwargs)
            if moba:
                ka.append(jnp.transpose(k.reshape(bsz, h_a, dh_a, t), (0, 3, 1, 2)))
                va.append(jnp.transpose(v.reshape(bsz, h_a, dh_a, t), (0, 3, 1, 2)))
            else:
                kd.append(jnp.transpose(k.reshape(bsz, h_b, 2, dh_b, t), (0, 4, 1, 2, 3)))
                vd.append(outs[2].reshape(bsz, t, h_b, 2 * dh_b))
        else:
            ck, cv = (cache_k_moba, cache_v_moba) if moba else (cache_k_diff, cache_v_diff)
            n_pool, page = ck.shape[1], ck.shape[2]
            v_token_minor = cv.shape[-1] < LANES
            ck = jnp.swapaxes(ck.reshape(ck.shape[0], n_pool, page, d), 2, 3)
            if v_token_minor:
                cv = jnp.swapaxes(cv.reshape(cv.shape[0], n_pool, page, d), 2, 3)
            else:
                assert cv.shape[-1] == LANES
                cv = cv.reshape(cv.shape[0], n_pool, page * (d // LANES), LANES)
            o = _paged_attn(q.reshape(bsz, t, d), k.reshape(bsz, t, d), v.reshape(bsz, t, d), ck, cv,
                            j, page_table, slopes, moba=moba, v_token_minor=v_token_minor,
                            **kwargs).reshape(bsz * t, d)
            if moba:
                ka.append(k.reshape(bsz, t, h_a, dh_a))
                va.append(v.reshape(bsz, t, h_a, dh_a))
            else:
                kd.append(k.reshape(bsz, t, h_b, 2, dh_b))
                vd.append(v.reshape(bsz, t, h_b, 2 * dh_b))
        branch = (o, z, w_out)
    y = _proj(xf, final_norm, branch=branch).reshape(bsz, t, d)
    return y, jnp.stack(ka), jnp.stack(va), jnp.stack(kd), jnp.stack(vd)


def kernel(x_prompt, x_sample, cache_k_moba, cache_v_moba, cache_k_diff, cache_v_diff, page_table, norm_a, w_in_a, w_out_a, norm_b, w_in_b, w_out_b, lambda_q1, lambda_k1, lambda_q2, lambda_k2, subln_b, final_norm):
    weights = (norm_a, w_in_a, w_out_a, norm_b, w_in_b, w_out_b, lambda_q1, lambda_k1, lambda_q2, lambda_k2,
               subln_b, final_norm)
    caches = (cache_k_moba, cache_v_moba, cache_k_diff, cache_v_diff)
    y_p, ka_p, va_p, kd_p, vd_p = _trunk(x_prompt, None, caches, weights)
    y_s, ka_s, va_s, kd_s, vd_s = _trunk(x_sample, page_table, caches, weights)
    return (y_p, y_s, ka_p, va_p, kd_p, vd_p, ka_s, va_s, kd_s, vd_s)
```

```python
import functools
import math

import jax
import jax.numpy as jnp
from jax import lax
from jax.experimental import pallas as pl
from jax.experimental.pallas import tpu as pltpu

MOBA_BLOCK = 256
MOBA_TOPK = 3
RMS_EPS = 1e-6
NEG = -1e30
LOG2E = math.log2(math.e)

LANES = 128
ROW_TILE = 256
KEY_GROUP = 2
N_SLOPE_PARTS = 3
ONES_ROWS = 16
PAGES_PER_STEP = 16
VMEM_LIMIT = 56 * 1024 * 1024

F32 = jnp.float32
BF16 = jnp.bfloat16


def _dot(a, b):
    return jnp.dot(a, b, preferred_element_type=F32)


def _dot_nt(a, b):
    return lax.dot_general(a, b, (((1,), (1,)), ((), ())), preferred_element_type=F32)


def _split(x):
    hi = x.astype(BF16)
    lo = (x - hi.astype(F32)).astype(BF16)
    return hi, lo


def _dot_3pass(a, b, dot=_dot):
    a_hi, a_lo = _split(a)
    b_hi, b_lo = _split(b)
    return dot(a_hi, b_hi) + (dot(a_hi, b_lo) + dot(a_lo, b_hi))


def _rms(x, g):
    return x * lax.rsqrt(jnp.mean(x * x, axis=-1, keepdims=True) + RMS_EPS) * g


def _alibi_slopes(n_heads):
    return 2.0 ** (-8.0 * jnp.arange(1, n_heads + 1, dtype=F32) / n_heads)


def _lambda_init(layer_idx):
    return 0.8 - 0.6 * math.exp(-0.3 * layer_idx)


def _reduce_rows(x, op):
    rows = x.shape[0]
    while rows > 8 and rows % 2 == 0:
        rows //= 2
        x = op(x[:rows], x[rows:])
    reduce = jnp.max if op is jnp.maximum else jnp.sum
    return reduce(x, axis=0, keepdims=True)


def _top_k_mask(gate, k, axis=1):
    col = lax.broadcasted_iota(jnp.int32, gate.shape, axis).astype(F32)
    sel = jnp.zeros(gate.shape, F32)
    for _ in range(k):
        mx = jnp.max(gate, axis=axis, keepdims=True)
        first = jnp.min(jnp.where(gate == mx, col, float(gate.shape[axis])), axis=axis, keepdims=True)
        hit = col == first
        sel = jnp.where(hit, jnp.where(mx > 0.5 * NEG, 1.0, 0.0), sel)
        gate = jnp.where(hit, -jnp.inf, gate)
    return sel


def _proj_kernel(*refs, has_branch, kinds):
    refs = list(refs)
    if has_branch:
        o_ref, z_ref, x_ref, w_out_ref = refs[:4]
        refs = refs[4:]
        z = z_ref[...]
        u = o_ref[...] * (z / (1.0 + jnp.exp(-z)))
        x = x_ref[...] + _dot(u.astype(BF16), w_out_ref[...])
    else:
        x = refs[0][...]
        refs = refs[1:]
    g_ref = refs[0]
    h = _rms(x, g_ref[...])
    if not kinds:
        refs[1][...] = h
        return
    w_refs = refs[1:1 + len(kinds)]
    outs = refs[1 + len(kinds):]
    if has_branch:
        outs[0][...] = x
        outs = outs[1:]
    hb = h.astype(BF16)
    tm = hb.shape[0]
    for kind, w_ref, out in zip(kinds, w_refs, outs):
        if kind == "rows":
            out[...] = _dot(hb, w_ref[...])
        elif kind == "cols":
            out[0] = _dot_nt(w_ref[...], hb)
        else:
            val = _dot(hb, w_ref[...])
            n_head = val.shape[1] // LANES
            for hd in range(n_head):
                out[pl.ds(hd, tm, stride=n_head), :] = val[:, hd * LANES:(hd + 1) * LANES]


def _proj(x, g, *, branch=None, w_in=None, outputs=((0, "rows"), (1, "rows"), (2, "rows"), (3, "rows")), batch=1):
    m, d = x.shape
    tm = min(ROW_TILE, m)
    assert m % tm == 0
    row = lambda i: (i, 0)
    fixed = lambda i: (0, 0)
    args, specs = [], []
    if branch is not None:
        o, z, w_out = branch
        args += [o, z, x, w_out.astype(BF16)]
        specs += [pl.BlockSpec((tm, o.shape[1]), row), pl.BlockSpec((tm, o.shape[1]), row),
                  pl.BlockSpec((tm, d), row), pl.BlockSpec(w_out.shape, fixed)]
    else:
        args.append(x)
        specs.append(pl.BlockSpec((tm, d), row))
    args.append(g.reshape(1, d))
    specs.append(pl.BlockSpec((1, d), fixed))
    kinds = ()
    if w_in is None:
        out_shape = jax.ShapeDtypeStruct((m, d), F32)
        out_specs = pl.BlockSpec((tm, d), row)
    else:
        kinds = tuple(kind for _, kind in outputs)
        da = w_in.shape[1] // 4
        seq = m // batch
        tiles_per_seq = max(seq // tm, 1)
        assert "cols" not in kinds or seq % tm == 0
        out_shape = [jax.ShapeDtypeStruct((m, d), F32)] if branch is not None else []
        out_specs = [pl.BlockSpec((tm, d), row)] if branch is not None else []
        for j, kind in outputs:
            w = w_in[:, j * da:(j + 1) * da].astype(BF16)
            args.append(w.T if kind == "cols" else w)
            specs.append(pl.BlockSpec(args[-1].shape, fixed))
            if kind == "rows":
                out_shape.append(jax.ShapeDtypeStruct((m, da), F32))
                out_specs.append(pl.BlockSpec((tm, da), row))
            elif kind == "cols":
                out_shape.append(jax.ShapeDtypeStruct((batch, da, seq), F32))
                out_specs.append(pl.BlockSpec((1, da, tm), lambda i: (i // tiles_per_seq, 0, i % tiles_per_seq)))
            else:
                assert kind == "tiles" and da % LANES == 0
                out_shape.append(jax.ShapeDtypeStruct((m * (da // LANES), LANES), F32))
                out_specs.append(pl.BlockSpec((tm * (da // LANES), LANES), row))
        out_shape, out_specs = tuple(out_shape), tuple(out_specs)
    return pl.pallas_call(
        functools.partial(_proj_kernel, has_branch=branch is not None, kinds=kinds),
        grid=(m // tm,),
        in_specs=specs,
        out_specs=out_specs,
        out_shape=out_shape,
        compiler_params=pltpu.CompilerParams(dimension_semantics=("arbitrary",), vmem_limit_bytes=VMEM_LIMIT),
        name="proj",
    )(*args)


def _prompt_attn_kernel(*refs, moba, n_blocks, group, scale, lam_init):
    if moba:
        slopes_ref, q_ref, k_ref, v_ref, o_ref, kb, vt, t_a, t_b, km, neg_scr = refs
    else:
        slopes_ref, lam_ref, subln_ref, q_ref, k_ref, v_ref, o_ref, kb, vt, t_a, t_b = refs
    blk = MOBA_BLOCK
    grp = pl.program_id(1)
    s = pl.program_id(2)
    half = LANES // 2

    @pl.when(s == 0)
    def _():
        pos = lax.broadcasted_iota(jnp.int32, (blk, LANES), 0)
        pos = jnp.where(lax.broadcasted_iota(jnp.int32, (blk, LANES), 1) < N_SLOPE_PARTS, pos, 0)
        key_pos = pos.astype(F32).astype(BF16)
        for n in range(n_blocks):
            kblk = k_ref[0, :, n * blk:(n + 1) * blk].T
            kb[n, :, :LANES] = kblk.astype(BF16)
            kb[n, :, LANES:] = key_pos
            vt[n] = v_ref[0, :, n * blk:(n + 1) * blk].astype(BF16)
            if moba:
                km[n:n + 1, :] = jnp.sum(kblk, axis=0, keepdims=True) * (1.0 / blk)

    tq = group * blk
    qt = q_ref[...].T
    ch = lax.broadcasted_iota(jnp.int32, (LANES, 1), 0)
    ch_masks = (ch < half, ch >= half)
    q_blk = lax.broadcasted_iota(jnp.int32, (1, tq), 1) // blk

    qts, slopes = [], []
    for e in range(2):
        slope = (slopes_ref[2 * grp + e] if moba else slopes_ref[grp]) * LOG2E
        qte = jnp.where(ch_masks[e], qt, 0.0)
        rest = jnp.full((LANES, tq), slope, F32)
        slope_rows = jnp.zeros((LANES, tq), F32)
        for part in range(N_SLOPE_PARTS):
            piece = rest.astype(BF16).astype(F32)
            slope_rows = jnp.where(ch == part, piece, slope_rows)
            rest = rest - piece
        qts.append(jnp.concatenate([(qte * (scale * LOG2E)).astype(BF16), slope_rows.astype(BF16)], axis=0))
        slopes.append(slope)
        if moba:
            gate = _dot_3pass(km[...], qte)
            past = lax.broadcasted_iota(jnp.int32, gate.shape, 0) < s * group + q_blk
            sel = _top_k_mask(jnp.where(past, gate, NEG), min(MOBA_TOPK, n_blocks), axis=0)
            neg = (1.0 - sel) * NEG
            for n in range(n_blocks):
                neg_scr[e, n] = neg[n:n + 1, :]

    n_val = half if moba else LANES

    def first_block(g):
        return jnp.minimum(g * group, n_blocks - group)

    def v_rows(g, e):
        n0 = first_block(g)
        vals = [vt[n0 + u, e * half:(e + 1) * half, :] if moba else vt[n0 + u] for u in range(group)]
        return jnp.concatenate([jnp.concatenate(vals, axis=1), jnp.ones((ONES_ROWS, tq), BF16)], axis=0)

    def block_maxima(t):
        return tuple(_reduce_rows(t[u * blk:(u + 1) * blk], jnp.maximum) for u in range(group))

    def store_scores(t_ref, g):
        kg = kb[pl.ds(first_block(g), group)].reshape(group * blk, 2 * LANES)
        maxima = []
        for e in range(2):
            t = _dot(kg, qts[e])
            t_ref[e] = t
            maxima.append(block_maxima(t))
        return tuple(maxima)

    def attend_group(t_ref, maxima, g, state):
        new_state = []
        for e in range(2):
            cs, block_max = [], []
            for u in range(group):
                c = slopes[e] * (((g - s) * group + u) * blk).astype(F32)
                if state is not None:
                    if moba:
                        c = c + neg_scr[e, first_block(g) + u]
                    c = jnp.where(g < s, c, NEG)
                cs.append(c)
                block_max.append(maxima[e][u] + c)
            m_new = block_max[0] if state is None else state[e][0]
            for bm in block_max[1 if state is None else 0:]:
                m_new = jnp.maximum(m_new, bm)
            p_all = jnp.concatenate(
                [jnp.exp2(t_ref[e, u * blk:(u + 1) * blk, :] - (m_new - cs[u])).astype(BF16) for u in range(group)],
                axis=0)
            acc = _dot(v_rows(g, e), p_all)
            if state is not None:
                acc = jnp.exp2(state[e][0] - m_new) * state[e][1] + acc
            new_state.append((m_new, acc))
        return tuple(new_state)

    store_scores(t_a, s)
    first_maxima = store_scores(t_b, 0)

    key = lax.broadcasted_iota(jnp.int32, (blk, tq), 0)
    qry = lax.broadcasted_iota(jnp.int32, (blk, tq), 1)
    for e in range(2):
        for u in range(group):
            other = neg_scr[e, s * group + u] if moba else jnp.where(q_blk > u, 0.0, NEG)
            mask = jnp.where(q_blk == u, jnp.where(key + u * blk <= qry, 0.0, NEG), other)
            t_a[e, u * blk:(u + 1) * blk, :] = t_a[e, u * blk:(u + 1) * blk, :] + mask
    state = attend_group(t_a, tuple(block_maxima(t_a[e]) for e in range(2)), s, None)

    def attend_pair(j, carry):
        state, maxima_b = carry
        maxima_a = store_scores(t_a, 2 * j + 1)
        state = attend_group(t_b, maxima_b, 2 * j, state)
        maxima_b = store_scores(t_b, 2 * j + 2)
        return attend_group(t_a, maxima_a, 2 * j + 1, state), maxima_b

    state, _ = lax.fori_loop(0, lax.div(s + 1, 2), attend_pair, (state, first_maxima))

    outs = [acc[:n_val] / acc[n_val:n_val + 1] for (_, acc) in state]
    if moba:
        o_ref[...] = jnp.concatenate(outs, axis=0).T
    else:
        o = (outs[0] - lam_ref[...] * outs[1]).T
        o_ref[...] = _rms(o, subln_ref[...]) * (1.0 - lam_init)


def _lambda_kernel(q1_ref, k1_ref, q2_ref, k2_ref, lam_ref, *, lam_init):
    a = jnp.sum(q1_ref[...] * k1_ref[...], axis=1, keepdims=True)
    b = jnp.sum(q2_ref[...] * k2_ref[...], axis=1, keepdims=True)
    lam_ref[...] = jnp.exp(a) - jnp.exp(b) + lam_init


def _lambda(lq1, lk1, lq2, lk2, lam_init):
    args = [v.reshape(1, -1) for v in (lq1, lk1, lq2, lk2)]
    return pl.pallas_call(
        functools.partial(_lambda_kernel, lam_init=lam_init),
        out_shape=jax.ShapeDtypeStruct((1, 1), F32),
        name="lambda",
    )(*args)


def _prompt_attn(q, k, v, slopes, *, batch, moba, lam=None, subln=None, lam_init=0.0):
    m, da = q.shape
    seq = m // batch
    blk = MOBA_BLOCK
    assert seq % blk == 0 and da % LANES == 0 and k.shape == v.shape == (batch, da, seq)
    n_groups = da // LANES
    n_blocks = seq // blk
    group = math.gcd(n_blocks, KEY_GROUP)
    tq = group * blk
    n_tiles = seq // tq
    q_spec = pl.BlockSpec((tq, LANES), lambda b, g, s: (b * n_tiles + s, g))
    kv_spec = pl.BlockSpec((1, LANES, seq), lambda b, g, s: (b, g, 0))
    smem = pl.BlockSpec(memory_space=pltpu.SMEM)
    args, specs = [slopes], [smem]
    scratch = [pltpu.VMEM((n_blocks, blk, 2 * LANES), BF16),
               pltpu.VMEM((n_blocks, LANES, blk), BF16),
               pltpu.VMEM((2, tq, tq), F32),
               pltpu.VMEM((2, tq, tq), F32)]
    if moba:
        scratch += [pltpu.VMEM((n_blocks, LANES), F32),
                    pltpu.VMEM((2, n_blocks, 1, tq), F32)]
    else:
        args += [lam, subln.reshape(1, LANES)]
        specs += [pl.BlockSpec((1, 1), lambda b, g, i: (0, 0)), pl.BlockSpec((1, LANES), lambda b, g, i: (0, 0))]
    args += [q, k, v]
    specs += [q_spec, kv_spec, kv_spec]
    return pl.pallas_call(
        functools.partial(_prompt_attn_kernel, moba=moba, n_blocks=n_blocks, group=group,
                          scale=(LANES // 2) ** -0.5, lam_init=lam_init),
        grid=(batch, n_groups, n_tiles),
        in_specs=specs,
        out_specs=q_spec,
        out_shape=jax.ShapeDtypeStruct((m, da), F32),
        scratch_shapes=scratch,
        compiler_params=pltpu.CompilerParams(dimension_semantics=("arbitrary",) * 3, vmem_limit_bytes=VMEM_LIMIT),
        name="moba_prompt" if moba else "diff_prompt",
    )(*args)


def _paged_attn_kernel(*refs, moba, v_token_minor, n_chunks, n_q, past_len, page, lam_init):
    pps = PAGES_PER_STEP
    pt_ref, slopes_ref = refs[:2]
    refs = refs[2:]
    if not moba:
        lam_ref, subln_ref = refs[:2]
        refs = refs[2:]
    q_ref, kn_ref, vn_ref = refs[:3]
    k_pages = refs[3:3 + pps]
    v_pages = refs[3 + pps:3 + 2 * pps]
    refs = refs[3 + 2 * pps:]
    o_ref, qbd_scr, s_scr, p_scr, pown_scr, acc_scr = refs[:6]
    if moba:
        kmean_scr = refs[6]
    del pt_ref
    step = pl.program_id(1)
    d = q_ref.shape[2]
    half = LANES // 2
    n_grp = d // half
    rows = n_q * n_grp
    chunk = pps * page
    pages_per_blk = MOBA_BLOCK // page
    grp_of_lane = lax.broadcasted_iota(jnp.int32, (n_grp, d), 1) // half
    grp_of_row = lax.broadcasted_iota(jnp.int32, (n_grp, d), 0)
    diag = grp_of_lane == grp_of_row

    @pl.when(step == 0)
    def _():
        q = q_ref[0]
        for t in range(n_q):
            qbd_scr[t * n_grp:(t + 1) * n_grp, :] = jnp.where(diag, jnp.broadcast_to(q[t:t + 1, :], (n_grp, d)), 0.0)
        if moba:
            kmean_scr[...] = jnp.zeros(kmean_scr.shape, F32)

    @pl.when(step < n_chunks)
    def _():
        pages = [ref[0, 0] for ref in k_pages]
        kc = jnp.concatenate([pg.astype(BF16) for pg in pages], axis=1)
        scale = half ** -0.5
        s_scr[step] = _dot((qbd_scr[...] * scale).astype(BF16), kc)
        if moba:
            blk_lane = lax.broadcasted_iota(jnp.int32, (1, LANES), 1)
            for a in range(pps // pages_per_blk):
                tot = pages[a * pages_per_blk]
                for b in range(1, pages_per_blk):
                    tot = tot + pages[a * pages_per_blk + b]
                mean = jnp.sum(tot, axis=1, keepdims=True) * (1.0 / MOBA_BLOCK)
                n = step * (pps // pages_per_blk) + a
                kmean_scr[...] = jnp.where(blk_lane == n, mean, kmean_scr[...])

    @pl.when(step == n_chunks - 1)
    def _():
        qbd = qbd_scr[...]
        row = lax.broadcasted_iota(jnp.int32, (rows, 1), 0)
        q_pos = past_len + row // n_grp
        slope = jnp.zeros((rows, 1), F32)
        for g in range(n_grp):
            s_g = slopes_ref[g] if moba else slopes_ref[g // 2]
            slope = jnp.where(row % n_grp == g, s_g, slope)
        key_in_chunk = lax.broadcasted_iota(jnp.int32, (rows, chunk), 1)
        if moba:
            n_blk = past_len // MOBA_BLOCK
            gate = _dot_3pass(qbd, kmean_scr[...])
            is_blk = lax.broadcasted_iota(jnp.int32, gate.shape, 1) < n_blk
            sel = _top_k_mask(jnp.where(is_blk, gate, NEG), min(MOBA_TOPK, n_blk)).astype(BF16)
            blk_per_chunk = chunk // MOBA_BLOCK
            e_row = lax.broadcasted_iota(jnp.int32, (LANES, chunk), 0)
            e_col = lax.broadcasted_iota(jnp.int32, (LANES, chunk), 1) // MOBA_BLOCK
        kn = kn_ref[0]
        s_own = []
        for t in range(n_q):
            s_t = jnp.sum(qbd * kn[t:t + 1, :], axis=1, keepdims=True) * (half ** -0.5)
            s_t = s_t - slope * (q_pos - (past_len + t)).astype(F32)
            s_own.append(jnp.where(q_pos >= past_len + t, s_t, NEG))
        m = s_own[0]
        for t in range(1, n_q):
            m = jnp.maximum(m, s_own[t])

        def biased(c):
            s = s_scr[c] - slope * (q_pos - (c * chunk + key_in_chunk)).astype(F32)
            if moba:
                expand = jnp.where(e_row == c * blk_per_chunk + e_col, 1.0, 0.0).astype(BF16)
                s = jnp.where(_dot(sel, expand) > 0.5, s, NEG)
            return s

        def max_body(c, m):
            s = biased(c)
            s_scr[c] = s
            return jnp.maximum(m, jnp.max(s, axis=1, keepdims=True))

        m = lax.fori_loop(0, n_chunks, max_body, m)

        def sum_body(c, l):
            p = jnp.exp(s_scr[c] - m)
            p_scr[c] = p.astype(BF16)
            return l + jnp.sum(p, axis=1, keepdims=True)

        p_own = [jnp.exp(s_t - m) for s_t in s_own]
        l = p_own[0]
        for t in range(1, n_q):
            l = l + p_own[t]
        l = lax.fori_loop(0, n_chunks, sum_body, l)
        for t in range(n_q):
            pown_scr[:, t:t + 1] = p_own[t]
        pown_scr[:, n_q:n_q + 1] = 1.0 / l

    @pl.when(step >= n_chunks)
    def _():
        if v_token_minor:
            vc = jnp.concatenate([ref[0, 0].astype(BF16) for ref in v_pages], axis=1)
            contrib = _dot_nt(p_scr[step - n_chunks], vc)
        else:
            n_head = d // LANES
            vc = jnp.concatenate(
                [jnp.concatenate([ref[0, 0, pl.ds(h, page, stride=n_head), :].astype(BF16) for h in range(n_head)],
                                 axis=1) for ref in v_pages], axis=0)
            contrib = _dot(p_scr[step - n_chunks], vc)

        @pl.when(step == n_chunks)
        def _():
            acc_scr[...] = contrib

        @pl.when(step > n_chunks)
        def _():
            acc_scr[...] += contrib

    @pl.when(step == 2 * n_chunks - 1)
    def _():
        vn = vn_ref[0]
        acc = acc_scr[...]
        for t in range(n_q):
            acc = acc + pown_scr[:, t:t + 1] * vn[t:t + 1, :]
        acc = acc * pown_scr[:, n_q:n_q + 1]
        if moba:
            for t in range(n_q):
                o_ref[0, t:t + 1, :] = jnp.sum(jnp.where(diag, acc[t * n_grp:(t + 1) * n_grp, :], 0.0),
                                               axis=0, keepdims=True)
        else:
            lane_head = lax.broadcasted_iota(jnp.int32, (n_grp, d), 1) // LANES
            row_grp = lax.broadcasted_iota(jnp.int32, (n_grp, d), 0)
            coef = jnp.where(row_grp % 2 == 0, 1.0, -lam_ref[...])
            coef = jnp.where(lane_head == row_grp // 2, coef, 0.0)
            subln = subln_ref[...]
            for t in range(n_q):
                o_t = jnp.sum(acc[t * n_grp:(t + 1) * n_grp, :] * coef, axis=0, keepdims=True)
                for h in range(d // LANES):
                    seg = o_t[:, h * LANES:(h + 1) * LANES]
                    o_ref[0, t:t + 1, h * LANES:(h + 1) * LANES] = _rms(seg, subln) * (1.0 - lam_init)


def _paged_attn(q, k_new, v_new, cache_k, cache_v, layer, page_table, slopes, *, moba, v_token_minor, lam=None,
                subln=None, lam_init=0.0):
    bd, n_q, d = q.shape
    page = cache_k.shape[3]
    n_pages = page_table.shape[1]
    past_len = n_pages * page
    pps = PAGES_PER_STEP
    assert n_pages % pps == 0 and MOBA_BLOCK % page == 0 and pps % (MOBA_BLOCK // page) == 0
    assert past_len % MOBA_BLOCK == 0 and n_q < LANES
    n_chunks = n_pages // pps
    chunk = pps * page
    half = LANES // 2
    n_grp = d // half
    rows = n_q * n_grp

    tok_spec = pl.BlockSpec((1, n_q, d), lambda b, s, pt: (b, 0, 0))

    def k_page_spec(p):
        return pl.BlockSpec((1, 1, d, page),
                            lambda b, s, pt: (layer, pt[b, jnp.minimum(s, n_chunks - 1) * pps + p], 0, 0))

    def v_page_spec(p):
        return pl.BlockSpec((1, 1, d, page) if v_token_minor else (1, 1, page * (d // LANES), LANES),
                            lambda b, s, pt: (layer, pt[b, jnp.maximum(s - n_chunks, 0) * pps + p], 0, 0))

    smem = pl.BlockSpec(memory_space=pltpu.SMEM)
    args, specs = [slopes], [smem]
    if not moba:
        args += [lam, subln.reshape(1, LANES)]
        specs += [pl.BlockSpec((1, 1), lambda b, s, pt: (0, 0)), pl.BlockSpec((1, LANES), lambda b, s, pt: (0, 0))]
    args += [q, k_new, v_new] + [cache_k] * pps + [cache_v] * pps
    specs += [tok_spec] * 3 + [k_page_spec(p) for p in range(pps)] + [v_page_spec(p) for p in range(pps)]
    scratch = [
        pltpu.VMEM((rows, d), F32),
        pltpu.VMEM((n_chunks, rows, chunk), F32),
        pltpu.VMEM((n_chunks, rows, chunk), BF16),
        pltpu.VMEM((rows, LANES), F32),
        pltpu.VMEM((rows, d), F32),
    ]
    if moba:
        assert past_len // MOBA_BLOCK <= LANES
        scratch.append(pltpu.VMEM((d, LANES), F32))
    grid_spec = pltpu.PrefetchScalarGridSpec(
        num_scalar_prefetch=1,
        grid=(bd, 2 * n_chunks),
        in_specs=specs,
        out_specs=tok_spec,
        scratch_shapes=scratch,
    )
    return pl.pallas_call(
        functools.partial(_paged_attn_kernel, moba=moba, v_token_minor=v_token_minor, n_chunks=n_chunks, n_q=n_q,
                          past_len=past_len, page=page, lam_init=lam_init),
        grid_spec=grid_spec,
        out_shape=jax.ShapeDtypeStruct((bd, n_q, d), F32),
        compiler_params=pltpu.CompilerParams(dimension_semantics=("arbitrary",) * 2, vmem_limit_bytes=VMEM_LIMIT),
        name="moba_sample" if moba else "diff_sample",
    )(page_table, *args)


def _trunk(x, page_table, caches, weights):
    (norm_a, w_in_a, w_out_a, norm_b, w_in_b, w_out_b, lq1, lk1, lq2, lk2, subln_b, final_norm) = weights
    cache_k_moba, cache_v_moba, cache_k_diff, cache_v_diff = caches
    bsz, t, d = x.shape
    depth = w_in_a.shape[0] + w_in_b.shape[0]
    h_a = d // (LANES // 2)
    h_b = d // LANES
    slopes_a = _alibi_slopes(h_a)
    slopes_b = _alibi_slopes(h_b)
    xf = x.reshape(bsz * t, d)
    ka, va, kd, vd = [], [], [], []
    branch = None
    for i in range(depth):
        j = i // 2
        moba = i % 2 == 0
        norm, w_in, w_out = (norm_a[j], w_in_a[j], w_out_a[j]) if moba else (norm_b[j], w_in_b[j], w_out_b[j])
        prompt = page_table is None
        dh_a, dh_b = d // h_a, d // (2 * h_b)
        if prompt:
            outputs = ((0, "rows"), (1, "cols"), (2, "cols"), (3, "rows")) if moba else \
                      ((0, "rows"), (1, "cols"), (2, "tiles"), (2, "cols"), (3, "rows"))
        else:
            outputs = ((0, "rows"), (1, "rows"), (2, "rows"), (3, "rows"))
        outs = _proj(xf, norm, branch=branch, w_in=w_in, outputs=outputs, batch=bsz)
        if branch is not None:
            xf = outs[0]
            outs = outs[1:]
        q, k, z = outs[0], outs[1], outs[-1]
        v = outs[-2]
        lam_init = _lambda_init(i)
        kwargs = {}
        if not moba:
            kwargs = dict(lam=_lambda(lq1[j], lk1[j], lq2[j], lk2[j], lam_init), subln=subln_b[j], lam_init=lam_init)
        slopes = slopes_a if moba else slopes_b
        if prompt:
            o = _prompt_attn(q, k, v, slopes, batch=bsz, moba=moba, **kwargs)
            if moba:
                ka.append(jnp.transpose(k.reshape(bsz, h_a, dh_a, t), (0, 3, 1, 2)))
                va.append(jnp.transpose(v.reshape(bsz, h_a, dh_a, t), (0, 3, 1, 2)))
            else:
                kd.append(jnp.transpose(k.reshape(bsz, h_b, 2, dh_b, t), (0, 4, 1, 2, 3)))
                vd.append(outs[2].reshape(bsz, t, h_b, 2 * dh_b))
        else:
            ck, cv = (cache_k_moba, cache_v_moba) if moba else (cache_k_diff, cache_v_diff)
            n_pool, page = ck.shape[1], ck.shape[2]
            v_token_minor = cv.shape[-1] < LANES
            ck = jnp.swapaxes(ck.reshape(ck.shape[0], n_pool, page, d), 2, 3)
            if v_token_minor:
                cv = jnp.swapaxes(cv.reshape(cv.shape[0], n_pool, page, d), 2, 3)
            else:
                assert cv.shape[-1] == LANES
                cv = cv.reshape(cv.shape[0], n_pool, page * (d // LANES), LANES)
            o = _paged_attn(q.reshape(bsz, t, d), k.reshape(bsz, t, d), v.reshape(bsz, t, d), ck, cv,
                            j, page_table, slopes, moba=moba, v_token_minor=v_token_minor,
                            **kwargs).reshape(bsz * t, d)
            if moba:
                ka.append(k.reshape(bsz, t, h_a, dh_a))
                va.append(v.reshape(bsz, t, h_a, dh_a))
            else:
                kd.append(k.reshape(bsz, t, h_b, 2, dh_b))
                vd.append(v.reshape(bsz, t, h_b, 2 * dh_b))
        branch = (o, z, w_out)
    y = _proj(xf, final_norm, branch=branch).reshape(bsz, t, d)
    return y, jnp.stack(ka), jnp.stack(va), jnp.stack(kd), jnp.stack(vd)


def kernel(x_prompt, x_sample, cache_k_moba, cache_v_moba, cache_k_diff, cache_v_diff, page_table, norm_a, w_in_a, w_out_a, norm_b, w_in_b, w_out_b, lambda_q1, lambda_k1, lambda_q2, lambda_k2, subln_b, final_norm):
    weights = (norm_a, w_in_a, w_out_a, norm_b, w_in_b, w_out_b, lambda_q1, lambda_k1, lambda_q2, lambda_k2,
               subln_b, final_norm)
    caches = (cache_k_moba, cache_v_moba, cache_k_diff, cache_v_diff)
    y_p, ka_p, va_p, kd_p, vd_p = _trunk(x_prompt, None, caches, weights)
    y_s, ka_s, va_s, kd_s, vd_s = _trunk(x_sample, page_table, caches, weights)
    return (y_p, y_s, ka_p, va_p, kd_p, vd_p, ka_s, va_s, kd_s, vd_s)
```

```python
import functools
import math

import jax
import jax.numpy as jnp
from jax import lax
from jax.experimental import pallas as pl
from jax.experimental.pallas import tpu as pltpu

MOBA_BLOCK = 256
MOBA_TOPK = 3
RMS_EPS = 1e-6
NEG = -1e30
LOG2E = math.log2(math.e)

LANES = 128
ROW_TILE = 256
KEY_GROUP = 2
N_SLOPE_PARTS = 3
ONES_ROWS = 16
PAGES_PER_STEP = 32
VMEM_LIMIT = 56 * 1024 * 1024

F32 = jnp.float32
BF16 = jnp.bfloat16


def _dot(a, b):
    return jnp.dot(a, b, preferred_element_type=F32)


def _dot_nt(a, b):
    return lax.dot_general(a, b, (((1,), (1,)), ((), ())), preferred_element_type=F32)


def _split(x):
    hi = x.astype(BF16)
    lo = (x - hi.astype(F32)).astype(BF16)
    return hi, lo


def _dot_3pass(a, b, dot=_dot):
    a_hi, a_lo = _split(a)
    b_hi, b_lo = _split(b)
    return dot(a_hi, b_hi) + (dot(a_hi, b_lo) + dot(a_lo, b_hi))


def _rms(x, g):
    return x * lax.rsqrt(jnp.mean(x * x, axis=-1, keepdims=True) + RMS_EPS) * g


def _alibi_slopes(n_heads):
    return 2.0 ** (-8.0 * jnp.arange(1, n_heads + 1, dtype=F32) / n_heads)


def _lambda_init(layer_idx):
    return 0.8 - 0.6 * math.exp(-0.3 * layer_idx)


def _reduce_rows(x, op):
    rows = x.shape[0]
    while rows > 8 and rows % 2 == 0:
        rows //= 2
        x = op(x[:rows], x[rows:])
    reduce = jnp.max if op is jnp.maximum else jnp.sum
    return reduce(x, axis=0, keepdims=True)


def _top_k_mask(gate, k, axis=1):
    col = lax.broadcasted_iota(jnp.int32, gate.shape, axis).astype(F32)
    sel = jnp.zeros(gate.shape, F32)
    for _ in range(k):
        mx = jnp.max(gate, axis=axis, keepdims=True)
        first = jnp.min(jnp.where(gate == mx, col, float(gate.shape[axis])), axis=axis, keepdims=True)
        hit = col == first
        sel = jnp.where(hit, jnp.where(mx > 0.5 * NEG, 1.0, 0.0), sel)
        gate = jnp.where(hit, -jnp.inf, gate)
    return sel


def _proj_kernel(*refs, has_branch, kinds):
    refs = list(refs)
    if has_branch:
        o_ref, z_ref, x_ref, w_out_ref = refs[:4]
        refs = refs[4:]
        z = z_ref[...]
        u = o_ref[...] * (z / (1.0 + jnp.exp(-z)))
        x = x_ref[...] + _dot(u.astype(BF16), w_out_ref[...])
    else:
        x = refs[0][...]
        refs = refs[1:]
    g_ref = refs[0]
    h = _rms(x, g_ref[...])
    if not kinds:
        refs[1][...] = h
        return
    w_refs = refs[1:1 + len(kinds)]
    outs = refs[1 + len(kinds):]
    if has_branch:
        outs[0][...] = x
        outs = outs[1:]
    hb = h.astype(BF16)
    tm = hb.shape[0]
    for kind, w_ref, out in zip(kinds, w_refs, outs):
        if kind == "rows":
            out[...] = _dot(hb, w_ref[...])
        elif kind == "cols":
            out[0] = _dot_nt(w_ref[...], hb)
        else:
            val = _dot(hb, w_ref[...])
            n_head = val.shape[1] // LANES
            for hd in range(n_head):
                out[pl.ds(hd, tm, stride=n_head), :] = val[:, hd * LANES:(hd + 1) * LANES]


def _proj(x, g, *, branch=None, w_in=None, outputs=((0, "rows"), (1, "rows"), (2, "rows"), (3, "rows")), batch=1):
    m, d = x.shape
    tm = min(ROW_TILE, m)
    assert m % tm == 0
    row = lambda i: (i, 0)
    fixed = lambda i: (0, 0)
    args, specs = [], []
    if branch is not None:
        o, z, w_out = branch
        args += [o, z, x, w_out.astype(BF16)]
        specs += [pl.BlockSpec((tm, o.shape[1]), row), pl.BlockSpec((tm, o.shape[1]), row),
                  pl.BlockSpec((tm, d), row), pl.BlockSpec(w_out.shape, fixed)]
    else:
        args.append(x)
        specs.append(pl.BlockSpec((tm, d), row))
    args.append(g.reshape(1, d))
    specs.append(pl.BlockSpec((1, d), fixed))
    kinds = ()
    if w_in is None:
        out_shape = jax.ShapeDtypeStruct((m, d), F32)
        out_specs = pl.BlockSpec((tm, d), row)
    else:
        kinds = tuple(kind for _, kind in outputs)
        da = w_in.shape[1] // 4
        seq = m // batch
        tiles_per_seq = max(seq // tm, 1)
        assert "cols" not in kinds or seq % tm == 0
        out_shape = [jax.ShapeDtypeStruct((m, d), F32)] if branch is not None else []
        out_specs = [pl.BlockSpec((tm, d), row)] if branch is not None else []
        for j, kind in outputs:
            w = w_in[:, j * da:(j + 1) * da].astype(BF16)
            args.append(w.T if kind == "cols" else w)
            specs.append(pl.BlockSpec(args[-1].shape, fixed))
            if kind == "rows":
                out_shape.append(jax.ShapeDtypeStruct((m, da), F32))
                out_specs.append(pl.BlockSpec((tm, da), row))
            elif kind == "cols":
                out_shape.append(jax.ShapeDtypeStruct((batch, da, seq), F32))
                out_specs.append(pl.BlockSpec((1, da, tm), lambda i: (i // tiles_per_seq, 0, i % tiles_per_seq)))
            else:
                assert kind == "tiles" and da % LANES == 0
                out_shape.append(jax.ShapeDtypeStruct((m * (da // LANES), LANES), F32))
                out_specs.append(pl.BlockSpec((tm * (da // LANES), LANES), row))
        out_shape, out_specs = tuple(out_shape), tuple(out_specs)
    return pl.pallas_call(
        functools.partial(_proj_kernel, has_branch=branch is not None, kinds=kinds),
        grid=(m // tm,),
        in_specs=specs,
        out_specs=out_specs,
        out_shape=out_shape,
        compiler_params=pltpu.CompilerParams(dimension_semantics=("arbitrary",), vmem_limit_bytes=VMEM_LIMIT),
        name="proj",
    )(*args)


def _prompt_attn_kernel(*refs, moba, n_blocks, group, scale, lam_init):
    if moba:
        slopes_ref, q_ref, k_ref, v_ref, o_ref, kb, vt, t_a, t_b, km, neg_scr = refs
    else:
        slopes_ref, lam_ref, subln_ref, q_ref, k_ref, v_ref, o_ref, kb, vt, t_a, t_b = refs
    blk = MOBA_BLOCK
    grp = pl.program_id(1)
    s = pl.program_id(2)
    half = LANES // 2

    @pl.when(s == 0)
    def _():
        pos = lax.broadcasted_iota(jnp.int32, (blk, LANES), 0)
        pos = jnp.where(lax.broadcasted_iota(jnp.int32, (blk, LANES), 1) < N_SLOPE_PARTS, pos, 0)
        key_pos = pos.astype(F32).astype(BF16)
        for n in range(n_blocks):
            kblk = k_ref[0, :, n * blk:(n + 1) * blk].T
            kb[n, :, :LANES] = kblk.astype(BF16)
            kb[n, :, LANES:] = key_pos
            vt[n] = v_ref[0, :, n * blk:(n + 1) * blk].astype(BF16)
            if moba:
                km[n:n + 1, :] = jnp.sum(kblk, axis=0, keepdims=True) * (1.0 / blk)

    tq = group * blk
    qt = q_ref[...].T
    ch = lax.broadcasted_iota(jnp.int32, (LANES, 1), 0)
    ch_masks = (ch < half, ch >= half)
    q_blk = lax.broadcasted_iota(jnp.int32, (1, tq), 1) // blk

    qts, slopes = [], []
    for e in range(2):
        slope = (slopes_ref[2 * grp + e] if moba else slopes_ref[grp]) * LOG2E
        qte = jnp.where(ch_masks[e], qt, 0.0)
        rest = jnp.full((LANES, tq), slope, F32)
        slope_rows = jnp.zeros((LANES, tq), F32)
        for part in range(N_SLOPE_PARTS):
            piece = rest.astype(BF16).astype(F32)
            slope_rows = jnp.where(ch == part, piece, slope_rows)
            rest = rest - piece
        qts.append(jnp.concatenate([(qte * (scale * LOG2E)).astype(BF16), slope_rows.astype(BF16)], axis=0))
        slopes.append(slope)
        if moba:
            gate = _dot_3pass(km[...], qte)
            past = lax.broadcasted_iota(jnp.int32, gate.shape, 0) < s * group + q_blk
            sel = _top_k_mask(jnp.where(past, gate, NEG), min(MOBA_TOPK, n_blocks), axis=0)
            neg = (1.0 - sel) * NEG
            for n in range(n_blocks):
                neg_scr[e, n] = neg[n:n + 1, :]

    n_val = half if moba else LANES

    def first_block(g):
        return jnp.minimum(g * group, n_blocks - group)

    def v_rows(g, e):
        n0 = first_block(g)
        vals = [vt[n0 + u, e * half:(e + 1) * half, :] if moba else vt[n0 + u] for u in range(group)]
        return jnp.concatenate([jnp.concatenate(vals, axis=1), jnp.ones((ONES_ROWS, tq), BF16)], axis=0)

    def block_maxima(t):
        return tuple(_reduce_rows(t[u * blk:(u + 1) * blk], jnp.maximum) for u in range(group))

    def store_scores(t_ref, g):
        kg = kb[pl.ds(first_block(g), group)].reshape(group * blk, 2 * LANES)
        maxima = []
        for e in range(2):
            t = _dot(kg, qts[e])
            t_ref[e] = t
            maxima.append(block_maxima(t))
        return tuple(maxima)

    def attend_group(t_ref, maxima, g, state):
        new_state = []
        for e in range(2):
            cs, block_max = [], []
            for u in range(group):
                c = slopes[e] * (((g - s) * group + u) * blk).astype(F32)
                if state is not None:
                    if moba:
                        c = c + neg_scr[e, first_block(g) + u]
                    c = jnp.where(g < s, c, NEG)
                cs.append(c)
                block_max.append(maxima[e][u] + c)
            m_new = block_max[0] if state is None else state[e][0]
            for bm in block_max[1 if state is None else 0:]:
                m_new = jnp.maximum(m_new, bm)
            p_all = jnp.concatenate(
                [jnp.exp2(t_ref[e, u * blk:(u + 1) * blk, :] - (m_new - cs[u])).astype(BF16) for u in range(group)],
                axis=0)
            acc = _dot(v_rows(g, e), p_all)
            if state is not None:
                acc = jnp.exp2(state[e][0] - m_new) * state[e][1] + acc
            new_state.append((m_new, acc))
        return tuple(new_state)

    store_scores(t_a, s)
    first_maxima = store_scores(t_b, 0)

    key = lax.broadcasted_iota(jnp.int32, (blk, tq), 0)
    qry = lax.broadcasted_iota(jnp.int32, (blk, tq), 1)
    for e in range(2):
        for u in range(group):
            other = neg_scr[e, s * group + u] if moba else jnp.where(q_blk > u, 0.0, NEG)
            mask = jnp.where(q_blk == u, jnp.where(key + u * blk <= qry, 0.0, NEG), other)
            t_a[e, u * blk:(u + 1) * blk, :] = t_a[e, u * blk:(u + 1) * blk, :] + mask
    state = attend_group(t_a, tuple(block_maxima(t_a[e]) for e in range(2)), s, None)

    def attend_pair(j, carry):
        state, maxima_b = carry
        maxima_a = store_scores(t_a, 2 * j + 1)
        state = attend_group(t_b, maxima_b, 2 * j, state)
        maxima_b = store_scores(t_b, 2 * j + 2)
        return attend_group(t_a, maxima_a, 2 * j + 1, state), maxima_b

    state, _ = lax.fori_loop(0, lax.div(s + 1, 2), attend_pair, (state, first_maxima))

    outs = [acc[:n_val] / acc[n_val:n_val + 1] for (_, acc) in state]
    if moba:
        o_ref[...] = jnp.concatenate(outs, axis=0).T
    else:
        o = (outs[0] - lam_ref[...] * outs[1]).T
        o_ref[...] = _rms(o, subln_ref[...]) * (1.0 - lam_init)


def _lambda_kernel(q1_ref, k1_ref, q2_ref, k2_ref, lam_ref, *, lam_init):
    a = jnp.sum(q1_ref[...] * k1_ref[...], axis=1, keepdims=True)
    b = jnp.sum(q2_ref[...] * k2_ref[...], axis=1, keepdims=True)
    lam_ref[...] = jnp.exp(a) - jnp.exp(b) + lam_init


def _lambda(lq1, lk1, lq2, lk2, lam_init):
    args = [v.reshape(1, -1) for v in (lq1, lk1, lq2, lk2)]
    return pl.pallas_call(
        functools.partial(_lambda_kernel, lam_init=lam_init),
        out_shape=jax.ShapeDtypeStruct((1, 1), F32),
        name="lambda",
    )(*args)


def _prompt_attn(q, k, v, slopes, *, batch, moba, lam=None, subln=None, lam_init=0.0):
    m, da = q.shape
    seq = m // batch
    blk = MOBA_BLOCK
    assert seq % blk == 0 and da % LANES == 0 and k.shape == v.shape == (batch, da, seq)
    n_groups = da // LANES
    n_blocks = seq // blk
    group = math.gcd(n_blocks, KEY_GROUP)
    tq = group * blk
    n_tiles = seq // tq
    q_spec = pl.BlockSpec((tq, LANES), lambda b, g, s: (b * n_tiles + s, g))
    kv_spec = pl.BlockSpec((1, LANES, seq), lambda b, g, s: (b, g, 0))
    smem = pl.BlockSpec(memory_space=pltpu.SMEM)
    args, specs = [slopes], [smem]
    scratch = [pltpu.VMEM((n_blocks, blk, 2 * LANES), BF16),
               pltpu.VMEM((n_blocks, LANES, blk), BF16),
               pltpu.VMEM((2, tq, tq), F32),
               pltpu.VMEM((2, tq, tq), F32)]
    if moba:
        scratch += [pltpu.VMEM((n_blocks, LANES), F32),
                    pltpu.VMEM((2, n_blocks, 1, tq), F32)]
    else:
        args += [lam, subln.reshape(1, LANES)]
        specs += [pl.BlockSpec((1, 1), lambda b, g, i: (0, 0)), pl.BlockSpec((1, LANES), lambda b, g, i: (0, 0))]
    args += [q, k, v]
    specs += [q_spec, kv_spec, kv_spec]
    return pl.pallas_call(
        functools.partial(_prompt_attn_kernel, moba=moba, n_blocks=n_blocks, group=group,
                          scale=(LANES // 2) ** -0.5, lam_init=lam_init),
        grid=(batch, n_groups, n_tiles),
        in_specs=specs,
        out_specs=q_spec,
        out_shape=jax.ShapeDtypeStruct((m, da), F32),
        scratch_shapes=scratch,
        compiler_params=pltpu.CompilerParams(dimension_semantics=("arbitrary",) * 3, vmem_limit_bytes=VMEM_LIMIT),
        name="moba_prompt" if moba else "diff_prompt",
    )(*args)


def _when(enabled, cond):
    return pl.when(cond) if enabled else (lambda fn: None)


def _paged_attn_kernel(*refs, scores_phase, moba, v_token_minor, pps, n_chunks, n_q, past_len, page, lam_init):
    refs = list(refs[1:])
    if scores_phase:
        slopes_ref, q_ref, kn_ref = refs[:3]
        k_pages = refs[3:3 + pps]
        p_out, pown_out, qbd_scr, s_scr = refs[3 + pps:7 + pps]
        p_scr, pown_scr = p_out.at[0], pown_out.at[0]
        if moba:
            kmean_scr = refs[7 + pps]
        d = q_ref.shape[2]
    else:
        if not moba:
            lam_ref, subln_ref = refs[:2]
            refs = refs[2:]
        p_in, pown_in, vn_ref = refs[:3]
        v_pages = refs[3:3 + pps]
        o_ref, acc_scr = refs[3 + pps:5 + pps]
        p_scr, pown_scr = p_in.at[0], pown_in.at[0]
        d = vn_ref.shape[2]
    step = pl.program_id(1)
    half = LANES // 2
    n_grp = d // half
    rows = n_q * n_grp
    chunk = pps * page
    pages_per_blk = MOBA_BLOCK // page
    grp_of_lane = lax.broadcasted_iota(jnp.int32, (n_grp, d), 1) // half
    grp_of_row = lax.broadcasted_iota(jnp.int32, (n_grp, d), 0)
    diag = grp_of_lane == grp_of_row

    @_when(scores_phase, step == 0)
    def _():
        q = q_ref[0]
        for t in range(n_q):
            qbd_scr[t * n_grp:(t + 1) * n_grp, :] = jnp.where(diag, jnp.broadcast_to(q[t:t + 1, :], (n_grp, d)), 0.0)
        if moba:
            kmean_scr[...] = jnp.zeros(kmean_scr.shape, F32)

    @_when(scores_phase, step < n_chunks)
    def _():
        pages = [ref[0, 0] for ref in k_pages]
        kc = jnp.concatenate([pg.astype(BF16) for pg in pages], axis=1)
        scale = half ** -0.5
        s_scr[step] = _dot((qbd_scr[...] * scale).astype(BF16), kc)
        if moba:
            blk_lane = lax.broadcasted_iota(jnp.int32, (1, LANES), 1)
            for a in range(pps // pages_per_blk):
                tot = pages[a * pages_per_blk]
                for b in range(1, pages_per_blk):
                    tot = tot + pages[a * pages_per_blk + b]
                mean = jnp.sum(tot, axis=1, keepdims=True) * (1.0 / MOBA_BLOCK)
                n = step * (pps // pages_per_blk) + a
                kmean_scr[...] = jnp.where(blk_lane == n, mean, kmean_scr[...])

    @_when(scores_phase, step == n_chunks - 1)
    def _():
        qbd = qbd_scr[...]
        row = lax.broadcasted_iota(jnp.int32, (rows, 1), 0)
        q_pos = past_len + row // n_grp
        slope = jnp.zeros((rows, 1), F32)
        for g in range(n_grp):
            s_g = slopes_ref[g] if moba else slopes_ref[g // 2]
            slope = jnp.where(row % n_grp == g, s_g, slope)
        key_in_chunk = lax.broadcasted_iota(jnp.int32, (rows, chunk), 1)
        if moba:
            n_blk = past_len // MOBA_BLOCK
            gate = _dot_3pass(qbd, kmean_scr[...])
            is_blk = lax.broadcasted_iota(jnp.int32, gate.shape, 1) < n_blk
            sel = _top_k_mask(jnp.where(is_blk, gate, NEG), min(MOBA_TOPK, n_blk)).astype(BF16)
            blk_per_chunk = chunk // MOBA_BLOCK
            e_row = lax.broadcasted_iota(jnp.int32, (LANES, chunk), 0)
            e_col = lax.broadcasted_iota(jnp.int32, (LANES, chunk), 1) // MOBA_BLOCK
        kn = kn_ref[0]
        s_own = []
        for t in range(n_q):
            s_t = jnp.sum(qbd * kn[t:t + 1, :], axis=1, keepdims=True) * (half ** -0.5)
            s_t = s_t - slope * (q_pos - (past_len + t)).astype(F32)
            s_own.append(jnp.where(q_pos >= past_len + t, s_t, NEG))
        m = s_own[0]
        for t in range(1, n_q):
            m = jnp.maximum(m, s_own[t])

        def biased(c):
            s = s_scr[c] - slope * (q_pos - (c * chunk + key_in_chunk)).astype(F32)
            if moba:
                expand = jnp.where(e_row == c * blk_per_chunk + e_col, 1.0, 0.0).astype(BF16)
                s = jnp.where(_dot(sel, expand) > 0.5, s, NEG)
            return s

        def max_body(c, m):
            s = biased(c)
            s_scr[c] = s
            return jnp.maximum(m, jnp.max(s, axis=1, keepdims=True))

        m = lax.fori_loop(0, n_chunks, max_body, m)

        def sum_body(c, l):
            p = jnp.exp(s_scr[c] - m)
            p_scr[c] = p.astype(BF16)
            return l + jnp.sum(p, axis=1, keepdims=True)

        p_own = [jnp.exp(s_t - m) for s_t in s_own]
        l = p_own[0]
        for t in range(1, n_q):
            l = l + p_own[t]
        l = lax.fori_loop(0, n_chunks, sum_body, l)
        pown_scr[...] = jnp.zeros(pown_scr.shape, F32)
        for t in range(n_q):
            pown_scr[:, t:t + 1] = p_own[t]
        pown_scr[:, n_q:n_q + 1] = 1.0 / l

    @_when(not scores_phase, step >= 0)
    def _():
        if v_token_minor:
            vc = jnp.concatenate([ref[0, 0].astype(BF16) for ref in v_pages], axis=1)
            contrib = _dot_nt(p_scr[step], vc)
        else:
            n_head = d // LANES
            vc = jnp.concatenate(
                [jnp.concatenate([ref[0, 0, pl.ds(h, page, stride=n_head), :].astype(BF16) for h in range(n_head)],
                                 axis=1) for ref in v_pages], axis=0)
            contrib = _dot(p_scr[step], vc)

        @pl.when(step == 0)
        def _():
            acc_scr[...] = contrib

        @pl.when(step > 0)
        def _():
            acc_scr[...] += contrib

    @_when(not scores_phase, step == n_chunks - 1)
    def _():
        vn = vn_ref[0]
        acc = acc_scr[...]
        for t in range(n_q):
            acc = acc + pown_scr[:, t:t + 1] * vn[t:t + 1, :]
        acc = acc * pown_scr[:, n_q:n_q + 1]
        if moba:
            for t in range(n_q):
                o_ref[0, t:t + 1, :] = jnp.sum(jnp.where(diag, acc[t * n_grp:(t + 1) * n_grp, :], 0.0),
                                               axis=0, keepdims=True)
        else:
            lane_head = lax.broadcasted_iota(jnp.int32, (n_grp, d), 1) // LANES
            row_grp = lax.broadcasted_iota(jnp.int32, (n_grp, d), 0)
            coef = jnp.where(row_grp % 2 == 0, 1.0, -lam_ref[...])
            coef = jnp.where(lane_head == row_grp // 2, coef, 0.0)
            subln = subln_ref[...]
            for t in range(n_q):
                o_t = jnp.sum(acc[t * n_grp:(t + 1) * n_grp, :] * coef, axis=0, keepdims=True)
                for h in range(d // LANES):
                    seg = o_t[:, h * LANES:(h + 1) * LANES]
                    o_ref[0, t:t + 1, h * LANES:(h + 1) * LANES] = _rms(seg, subln) * (1.0 - lam_init)


def _paged_attn(q, k_new, v_new, cache_k, cache_v, layer, page_table, slopes, *, moba, v_token_minor, lam=None,
                subln=None, lam_init=0.0):
    bd, n_q, d = q.shape
    page = cache_k.shape[3]
    n_pages = page_table.shape[1]
    past_len = n_pages * page
    pps = math.gcd(n_pages, PAGES_PER_STEP)
    assert MOBA_BLOCK % page == 0 and pps % (MOBA_BLOCK // page) == 0
    assert past_len % MOBA_BLOCK == 0 and n_q < LANES
    n_chunks = n_pages // pps
    chunk = pps * page
    half = LANES // 2
    n_grp = d // half
    rows = n_q * n_grp

    tok_spec = pl.BlockSpec((1, n_q, d), lambda b, s, pt: (b, 0, 0))
    p_spec = pl.BlockSpec((1, n_chunks, rows, chunk), lambda b, s, pt: (b, 0, 0, 0))
    pown_spec = pl.BlockSpec((1, rows, LANES), lambda b, s, pt: (b, 0, 0))
    k_block = (1, 1, d, page)
    v_block = k_block if v_token_minor else (1, 1, page * (d // LANES), LANES)

    def page_spec(block, p):
        return pl.BlockSpec(block, lambda b, s, pt: (layer, pt[b, s * pps + p], 0, 0))

    smem = pl.BlockSpec(memory_space=pltpu.SMEM)
    kernel = functools.partial(_paged_attn_kernel, moba=moba, v_token_minor=v_token_minor, pps=pps,
                               n_chunks=n_chunks, n_q=n_q, past_len=past_len, page=page, lam_init=lam_init)
    params = pltpu.CompilerParams(dimension_semantics=("arbitrary",) * 2, vmem_limit_bytes=VMEM_LIMIT)
    name = "moba_sample" if moba else "diff_sample"

    scratch = [pltpu.VMEM((rows, d), F32),
               pltpu.VMEM((n_chunks, rows, chunk), F32)]
    if moba:
        assert past_len // MOBA_BLOCK <= LANES
        scratch.append(pltpu.VMEM((d, LANES), F32))
    probs, probs_new = pl.pallas_call(
        functools.partial(kernel, scores_phase=True),
        grid_spec=pltpu.PrefetchScalarGridSpec(
            num_scalar_prefetch=1,
            grid=(bd, n_chunks),
            in_specs=[smem, tok_spec, tok_spec] + [page_spec(k_block, p) for p in range(pps)],
            out_specs=(p_spec, pown_spec),
            scratch_shapes=scratch,
        ),
        out_shape=(jax.ShapeDtypeStruct((bd, n_chunks, rows, chunk), BF16),
                   jax.ShapeDtypeStruct((bd, rows, LANES), F32)),
        compiler_params=params,
        name=name + "_scores",
    )(page_table, slopes, q, k_new, *([cache_k] * pps))

    args, specs = [], []
    if not moba:
        args += [lam, subln.reshape(1, LANES)]
        specs += [pl.BlockSpec((1, 1), lambda b, s, pt: (0, 0)), pl.BlockSpec((1, LANES), lambda b, s, pt: (0, 0))]
    args += [probs, probs_new, v_new] + [cache_v] * pps
    specs += [p_spec, pown_spec, tok_spec] + [page_spec(v_block, p) for p in range(pps)]
    return pl.pallas_call(
        functools.partial(kernel, scores_phase=False),
        grid_spec=pltpu.PrefetchScalarGridSpec(
            num_scalar_prefetch=1,
            grid=(bd, n_chunks),
            in_specs=specs,
            out_specs=tok_spec,
            scratch_shapes=[pltpu.VMEM((rows, d), F32)],
        ),
        out_shape=jax.ShapeDtypeStruct((bd, n_q, d), F32),
        compiler_params=params,
        name=name + "_values",
    )(page_table, *args)


def _trunk(x, page_table, caches, weights):
    (norm_a, w_in_a, w_out_a, norm_b, w_in_b, w_out_b, lq1, lk1, lq2, lk2, subln_b, final_norm) = weights
    cache_k_moba, cache_v_moba, cache_k_diff, cache_v_diff = caches
    bsz, t, d = x.shape
    depth = w_in_a.shape[0] + w_in_b.shape[0]
    h_a = d // (LANES // 2)
    h_b = d // LANES
    slopes_a = _alibi_slopes(h_a)
    slopes_b = _alibi_slopes(h_b)
    xf = x.reshape(bsz * t, d)
    ka, va, kd, vd = [], [], [], []
    branch = None
    for i in range(depth):
        j = i // 2
        moba = i % 2 == 0
        norm, w_in, w_out = (norm_a[j], w_in_a[j], w_out_a[j]) if moba else (norm_b[j], w_in_b[j], w_out_b[j])
        prompt = page_table is None
        dh_a, dh_b = d // h_a, d // (2 * h_b)
        if prompt:
            outputs = ((0, "rows"), (1, "cols"), (2, "cols"), (3, "rows")) if moba else \
                      ((0, "rows"), (1, "cols"), (2, "tiles"), (2, "cols"), (3, "rows"))
        else:
            outputs = ((0, "rows"), (1, "rows"), (2, "rows"), (3, "rows"))
        outs = _proj(xf, norm, branch=branch, w_in=w_in, outputs=outputs, batch=bsz)
        if branch is not None:
            xf = outs[0]
            outs = outs[1:]
        q, k, z = outs[0], outs[1], outs[-1]
        v = outs[-2]
        lam_init = _lambda_init(i)
        kwargs = {}
        if not moba:
            kwargs = dict(lam=_lambda(lq1[j], lk1[j], lq2[j], lk2[j], lam_init), subln=subln_b[j], lam_init=lam_init)
        slopes = slopes_a if moba else slopes_b
        if prompt:
            o = _prompt_attn(q, k, v, slopes, batch=bsz, moba=moba, **kwargs)
            if moba:
                ka.append(jnp.transpose(k.reshape(bsz, h_a, dh_a, t), (0, 3, 1, 2)))
                va.append(jnp.transpose(v.reshape(bsz, h_a, dh_a, t), (0, 3, 1, 2)))
            else:
                kd.append(jnp.transpose(k.reshape(bsz, h_b, 2, dh_b, t), (0, 4, 1, 2, 3)))
                vd.append(outs[2].reshape(bsz, t, h_b, 2 * dh_b))
        else:
            ck, cv = (cache_k_moba, cache_v_moba) if moba else (cache_k_diff, cache_v_diff)
            n_pool, page = ck.shape[1], ck.shape[2]
            v_token_minor = cv.shape[-1] < LANES
            ck = jnp.swapaxes(ck.reshape(ck.shape[0], n_pool, page, d), 2, 3)
            if v_token_minor:
                cv = jnp.swapaxes(cv.reshape(cv.shape[0], n_pool, page, d), 2, 3)
            else:
                assert cv.shape[-1] == LANES
                cv = cv.reshape(cv.shape[0], n_pool, page * (d // LANES), LANES)
            o = _paged_attn(q.reshape(bsz, t, d), k.reshape(bsz, t, d), v.reshape(bsz, t, d), ck, cv,
                            j, page_table, slopes, moba=moba, v_token_minor=v_token_minor,
                            **kwargs).reshape(bsz * t, d)
            if moba:
                ka.append(k.reshape(bsz, t, h_a, dh_a))
                va.append(v.reshape(bsz, t, h_a, dh_a))
            else:
                kd.append(k.reshape(bsz, t, h_b, 2, dh_b))
                vd.append(v.reshape(bsz, t, h_b, 2 * dh_b))
        branch = (o, z, w_out)
    y = _proj(xf, final_norm, branch=branch).reshape(bsz, t, d)
    return y, jnp.stack(ka), jnp.stack(va), jnp.stack(kd), jnp.stack(vd)


def kernel(x_prompt, x_sample, cache_k_moba, cache_v_moba, cache_k_diff, cache_v_diff, page_table, norm_a, w_in_a, w_out_a, norm_b, w_in_b, w_out_b, lambda_q1, lambda_k1, lambda_q2, lambda_k2, subln_b, final_norm):
    weights = (norm_a, w_in_a, w_out_a, norm_b, w_in_b, w_out_b, lambda_q1, lambda_k1, lambda_q2, lambda_k2,
               subln_b, final_norm)
    caches = (cache_k_moba, cache_v_moba, cache_k_diff, cache_v_diff)
    y_p, ka_p, va_p, kd_p, vd_p = _trunk(x_prompt, None, caches, weights)
    y_s, ka_s, va_s, kd_s, vd_s = _trunk(x_sample, page_table, caches, weights)
    return (y_p, y_s, ka_p, va_p, kd_p, vd_p, ka_s, va_s, kd_s, vd_s)
```
